```python
import functools
import jax, jax.numpy as jnp
from jax import lax
import numpy as np

D_MODEL = 4096
BATCH = 32
SEQ = 256
DEPTH = 1
DEC_BATCH = 8
DEC_SEQ = 4096
PAST_LEN = 256

GRID_W = 64
D_MIX = D_MODEL
CONV_W = D_MIX // 2
CONV_K = 31
N_HEADS = 16
HEAD_DIM = (D_MIX - CONV_W) // N_HEADS
ATT_W = N_HEADS * HEAD_DIM
IN_COLS = 2 * CONV_W + 3 * ATT_W
WIN_R = 8
WIN_C = 16
Q_BLOCK = 128
N_GROUPS = 4
EXPERTS_PER_GROUP = 8
N_EXPERTS = N_GROUPS * EXPERTS_PER_GROUP
TOP_K_IN_GROUP = 2
D_EXPERT = D_MODEL // 8
N_MOD = 6
EPS = 1e-6

kernel_name = "hymba_conformer_natten_hmoe_diffusion_step"


def rms_norm(x, g):
    xf = x.astype(jnp.float32)
    y = xf * lax.rsqrt(jnp.mean(xf * xf, axis=-1, keepdims=True) + EPS)
    return (y * g.astype(jnp.float32)).astype(x.dtype)


def layer_norm(x, g, b):
    xf = x.astype(jnp.float32)
    mu = jnp.mean(xf, axis=-1, keepdims=True)
    var = jnp.mean(jnp.square(xf - mu), axis=-1, keepdims=True)
    y = (xf - mu) * lax.rsqrt(var + EPS)
    return (y * g.astype(jnp.float32) + b.astype(jnp.float32)).astype(x.dtype)


def ada_modulation(cond, w_mod, b_mod):
    m = jax.nn.silu(cond) @ w_mod + b_mod
    return jnp.split(m[:, None, :], N_MOD, axis=-1)


def depthwise_conv(u, w, b):
    ch = u.shape[-1]
    y = lax.conv_general_dilated(u, w[:, None, :], window_strides=(1,),
                                 padding=[(CONV_K // 2, CONV_K // 2)],
                                 dimension_numbers=('NWC', 'WIO', 'NWC'),
                                 feature_group_count=ch)
    return y + b


def mixer_branches(h, w_in, conv_dw, conv_b, conv_ln_g, conv_ln_b, q_norm_g, k_norm_g):
    bsz, s, _ = h.shape
    proj = h @ w_in
    a, gt, q, k, v = jnp.split(
        proj, [CONV_W, 2 * CONV_W, 2 * CONV_W + ATT_W, 2 * CONV_W + 2 * ATT_W], axis=-1)
    u = a * jax.nn.sigmoid(gt)
    u = jax.nn.silu(layer_norm(depthwise_conv(u, conv_dw, conv_b), conv_ln_g, conv_ln_b))
    q = rms_norm(q.reshape(bsz, s, N_HEADS, HEAD_DIM), q_norm_g)
    k = rms_norm(k.reshape(bsz, s, N_HEADS, HEAD_DIM), k_norm_g)
    v = v.reshape(bsz, s, N_HEADS, HEAD_DIM)
    return u, q, k, v


def context_self_attention(q, k, v):
    bsz, length, nh, dh = q.shape
    nb = length // Q_BLOCK
    qb = q.reshape(bsz, nb, Q_BLOCK, nh, dh).transpose(1, 0, 2, 3, 4)
    scale = HEAD_DIM ** -0.5

    def block(qi):
        s = jnp.einsum('bqhd,bkhd->bhqk', qi, k).astype(jnp.float32) * scale
        p = jax.nn.softmax(s, axis=-1).astype(v.dtype)
        return jnp.einsum('bhqk,bkhd->bqhd', p, v)

    o = lax.map(block, qb)
    return o.transpose(1, 0, 2, 3, 4).reshape(bsz, length, nh * dh)


def neighbourhood_attention(q, k, v, k_ctx, v_ctx, rpb):
    bsz, s, nh, dh = q.shape
    rows = s // GRID_W
    kr = min(WIN_R, rows)
    kc = WIN_C
    qg = q.reshape(bsz, rows, GRID_W, nh, dh)
    kg = k.reshape(bsz, rows, GRID_W, nh, dh)
    vg = v.reshape(bsz, rows, GRID_W, nh, dh)
    cols = np.arange(GRID_W)
    c_start = np.clip(cols - kc // 2, 0, GRID_W - kc)
    col_mask = (cols[None, :] >= c_start[:, None]) & (cols[None, :] < c_start[:, None] + kc)
    lat_mask = np.broadcast_to(col_mask[:, None, :], (GRID_W, kr, GRID_W)).reshape(GRID_W, kr * GRID_W)
    dc_idx = np.clip(cols[None, :] - cols[:, None], -(WIN_C - 1), WIN_C - 1) + WIN_C - 1
    n_lat = kr * GRID_W
    scale = HEAD_DIM ** -0.5

    def row_block(r):
        start = jnp.clip(r - kr // 2, 0, rows - kr)
        q_blk = lax.dynamic_index_in_dim(qg, r, axis=1, keepdims=False)
        k_blk = lax.dynamic_slice_in_dim(kg, start, kr, axis=1).reshape(bsz, n_lat, nh, dh)
        v_blk = lax.dynamic_slice_in_dim(vg, start, kr, axis=1).reshape(bsz, n_lat, nh, dh)
        dr_idx = start + jnp.arange(kr) - r + WIN_R - 1
        bias = rpb[:, dr_idx][:, :, dc_idx]
        bias = bias.transpose(0, 2, 1, 3).reshape(nh, GRID_W, n_lat)
        s_lat = jnp.einsum('bqhd,bkhd->bhqk', q_blk, k_blk).astype(jnp.float32) * scale
        s_lat = jnp.where(lat_mask, s_lat + bias.astype(jnp.float32), -jnp.inf)
        s_ctx = jnp.einsum('bqhd,bhkd->bhqk', q_blk, k_ctx).astype(jnp.float32) * scale
        p = jax.nn.softmax(jnp.concatenate([s_lat, s_ctx], axis=-1), axis=-1).astype(v.dtype)
        return (jnp.einsum('bhqk,bkhd->bqhd', p[..., :n_lat], v_blk)
                + jnp.einsum('bhqk,bhkd->bqhd', p[..., n_lat:], v_ctx))

    o = lax.map(row_block, jnp.arange(rows))
    return o.transpose(1, 0, 2, 3, 4).reshape(bsz, s, nh * dh)


def hierarchical_moe(h, w_rg, b_rg, w_re, b_re, w_gate, w_up, w_down):
    shape = h.shape
    t = h.reshape(-1, shape[-1])
    n_tok = t.shape[0]
    p_grp = jax.nn.softmax((t @ w_rg + b_rg).astype(jnp.float32), axis=-1)
    g_prob, g_idx = lax.top_k(p_grp, 1)
    e_logits = (t @ w_re + b_re).astype(jnp.float32).reshape(n_tok, N_GROUPS, EXPERTS_PER_GROUP)
    e_sel = jnp.take_along_axis(e_logits, g_idx[:, :, None], axis=1)[:, 0]
    e_val, e_idx = lax.top_k(jax.nn.softmax(e_sel, axis=-1), TOP_K_IN_GROUP)
    comb = g_prob * e_val / jnp.sum(e_val, axis=-1, keepdims=True)
    expert_id = g_idx * EXPERTS_PER_GROUP + e_idx
    gates = jnp.sum(jax.nn.one_hot(expert_id, N_EXPERTS, dtype=jnp.float32) * comb[..., None], axis=1)
    gates = gates.astype(t.dtype)
    out = jnp.zeros_like(t)
    for g in range(N_GROUPS):
        sl = slice(g * EXPERTS_PER_GROUP, (g + 1) * EXPERTS_PER_GROUP)
        a = jnp.einsum('td,edf->tef', t, w_gate[sl])
        u = jnp.einsum('td,edf->tef', t, w_up[sl])
        hid = jax.nn.silu(a) * u * gates[:, sl, None]
        out = out + jnp.einsum('tef,efd->td', hid, w_down[sl])
    return out.reshape(shape)


def trunk_layer(x, cond, attend, norm1_g, w_mod, b_mod, w_in, conv_dw, conv_b, conv_ln_g,
                conv_ln_b, q_norm_g, k_norm_g, w_out, norm2_g, w_rg, b_rg, w_re, b_re,
                w_gate, w_up, w_down):
    sh1, sc1, g1, sh2, sc2, g2 = ada_modulation(cond, w_mod, b_mod)
    h = rms_norm(x, norm1_g) * (1 + sc1) + sh1
    u, q, k, v = mixer_branches(h, w_in, conv_dw, conv_b, conv_ln_g, conv_ln_b, q_norm_g, k_norm_g)
    att = attend(q, k, v)
    x = x + g1 * (jnp.concatenate([u, att], axis=-1) @ w_out)
    h = rms_norm(x, norm2_g) * (1 + sc2) + sh2
    x = x + g2 * hierarchical_moe(h, w_rg, b_rg, w_re, b_re, w_gate, w_up, w_down)
    return x, k, v


def setup_inputs(seed: int = 0) -> dict:
    key = jax.random.key(seed)
    ks = jax.random.split(key, 32)
    f32 = jnp.float32
    nrm = lambda i, shape, s: jax.random.normal(ks[i], shape, f32) * s
    inv = D_MODEL ** -0.5
    return {
        "x_prompt": nrm(0, (BATCH, SEQ, D_MODEL), 1.0),
        "x_sample": nrm(1, (DEC_BATCH, DEC_SEQ, D_MODEL), 1.0),
        "cache_k": nrm(2, (DEC_BATCH, DEPTH, N_HEADS, PAST_LEN, HEAD_DIM), 1.0),
        "cache_v": nrm(3, (DEC_BATCH, DEPTH, N_HEADS, PAST_LEN, HEAD_DIM), 1.0),
        "c": nrm(4, (DEC_BATCH, D_MODEL), 1.0),
        "c_ctx": nrm(5, (D_MODEL,), 1.0),
        "norm1_g": 1.0 + nrm(6, (DEPTH, D_MODEL), 0.02),
        "w_mod": nrm(7, (DEPTH, D_MODEL, N_MOD * D_MODEL), 0.5 * inv),
        "b_mod": nrm(8, (DEPTH, N_MOD * D_MODEL), 0.02),
        "w_in": nrm(9, (DEPTH, D_MODEL, IN_COLS), inv),
        "conv_dw": nrm(10, (DEPTH, CONV_K, CONV_W), CONV_K ** -0.5),
        "conv_b": nrm(11, (DEPTH, CONV_W), 0.02),
        "conv_ln_g": 1.0 + nrm(12, (DEPTH, CONV_W), 0.02),
        "conv_ln_b": nrm(13, (DEPTH, CONV_W), 0.02),
        "q_norm_g": 1.0 + nrm(14, (DEPTH, HEAD_DIM), 0.02),
        "k_norm_g": 1.0 + nrm(15, (DEPTH, HEAD_DIM), 0.02),
        "rpb": nrm(16, (DEPTH, N_HEADS, 2 * WIN_R - 1, 2 * WIN_C - 1), 0.5),
        "w_out": nrm(17, (DEPTH, D_MIX, D_MODEL), D_MIX ** -0.5),
        "norm2_g": 1.0 + nrm(18, (DEPTH, D_MODEL), 0.02),
        "w_rg": nrm(19, (DEPTH, D_MODEL, N_GROUPS), inv),
        "b_rg": nrm(20, (DEPTH, N_GROUPS), 0.01),
        "w_re": nrm(21, (DEPTH, D_MODEL, N_EXPERTS), inv),
        "b_re": nrm(22, (DEPTH, N_EXPERTS), 0.01),
        "w_gate": nrm(23, (DEPTH, N_EXPERTS, D_MODEL, D_EXPERT), inv),
        "w_up": nrm(24, (DEPTH, N_EXPERTS, D_MODEL, D_EXPERT), inv),
        "w_down": nrm(25, (DEPTH, N_EXPERTS, D_EXPERT, D_MODEL), D_EXPERT ** -0.5),
    }


def reference(x_prompt, x_sample, cache_k, cache_v, c, c_ctx, norm1_g, w_mod, b_mod, w_in,
              conv_dw, conv_b, conv_ln_g, conv_ln_b, q_norm_g, k_norm_g, rpb, w_out, norm2_g,
              w_rg, b_rg, w_re, b_re, w_gate, w_up, w_down):
    y_prompt = x_prompt
    y_sample = x_sample
    k_list = []
    v_list = []
    for l in range(DEPTH):
        shared = (norm1_g[l], w_mod[l], b_mod[l], w_in[l], conv_dw[l], conv_b[l], conv_ln_g[l],
                  conv_ln_b[l], q_norm_g[l], k_norm_g[l], w_out[l], norm2_g[l], w_rg[l], b_rg[l],
                  w_re[l], b_re[l], w_gate[l], w_up[l], w_down[l])
        y_prompt, k_ctx, v_ctx = trunk_layer(y_prompt, c_ctx[None, :], context_self_attention, *shared)
        k_list.append(k_ctx.transpose(0, 2, 1, 3))
        v_list.append(v_ctx.transpose(0, 2, 1, 3))
        attend = functools.partial(neighbourhood_attention, k_ctx=cache_k[:, l], v_ctx=cache_v[:, l], rpb=rpb[l])
        y_sample, _, _ = trunk_layer(y_sample, c, attend, *shared)
    new_cache_k = jnp.stack(k_list, axis=1)
    new_cache_v = jnp.stack(v_list, axis=1)
    return (y_prompt, y_sample, new_cache_k, new_cache_v)
```

```python
import functools

import numpy as np
import jax
import jax.numpy as jnp
from jax import lax
from jax.experimental import pallas as pl
from jax.experimental.pallas import tpu as pltpu

F32 = jnp.float32
BF16 = jnp.bfloat16

GRID_W = 64
WIN_R = 8
WIN_C = 16
CONV_K = 31
HEAD_DIM = 128
N_GROUPS = 4
EXPERTS_PER_GROUP = 8
N_EXPERTS = N_GROUPS * EXPERTS_PER_GROUP
N_MOD = 6
EPS = 1e-6
MASK_VALUE = -1e30

LANES = 128
SUBLANES = 8
MIB = 1024 * 1024

COND_ROWS = 16
MOD_TN = 512
NORM_TM = 256
PROJ_TM = 1024
PROJ_HEADS = 4
CONV_TS = 256
CONV_HALO = 16
CONV_RC = 64
MOE_TM = 256
COMB_TM = 128
ROUTE_COLS = LANES


def _params(n_axes, vmem_mib):
    return pltpu.CompilerParams(
        dimension_semantics=("arbitrary",) * n_axes,
        vmem_limit_bytes=vmem_mib * MIB,
    )


def _sigmoid(x):
    return 1.0 / (1.0 + jnp.exp(-x))


def _dot(a, b):
    return jnp.dot(a, b, preferred_element_type=F32)


def _dot_t(a, b):
    return lax.dot_general(a, b, (((1,), (1,)), ((), ())), preferred_element_type=F32)


def _mod_kernel(c_ref, w_ref, b_ref, o_ref):
    c = c_ref[...]
    s = (c * _sigmoid(c)).astype(BF16)
    o_ref[...] = _dot(s, w_ref[...].astype(BF16)) + b_ref[...]


def _modulation(cond, w_mod, b_mod):
    d, n = w_mod.shape
    tn = min(MOD_TN, n)
    return pl.pallas_call(
        _mod_kernel,
        grid=(n // tn,),
        in_specs=[
            pl.BlockSpec((COND_ROWS, d), lambda j: (0, 0)),
            pl.BlockSpec((d, tn), lambda j: (0, j)),
            pl.BlockSpec((1, tn), lambda j: (0, j)),
        ],
        out_specs=pl.BlockSpec((COND_ROWS, tn), lambda j: (0, j)),
        out_shape=jax.ShapeDtypeStruct((COND_ROWS, n), F32),
        compiler_params=_params(1, 40),
        name="modulation",
    )(cond, w_mod, b_mod)


def _rms_mod(x, g, sc, sh):
    y = x * lax.rsqrt(jnp.mean(x * x, axis=-1, keepdims=True) + EPS) * g
    return y * (1.0 + sc) + sh


def _norm_mod_kernel(x_ref, g_ref, sc_ref, sh_ref, o_ref):
    o_ref[...] = _rms_mod(x_ref[...], g_ref[...], sc_ref[0], sh_ref[0]).astype(o_ref.dtype)


def _mod_spec(d, row_of_tile, which):
    return pl.BlockSpec((1, 1, d), lambda i, *_: (row_of_tile(i) * N_MOD + which, 0, 0))


def _norm_mod(x2d, gain, mod, row_of_tile, sc_idx, sh_idx, tm):
    m, d = x2d.shape
    return pl.pallas_call(
        _norm_mod_kernel,
        grid=(m // tm,),
        in_specs=[
            pl.BlockSpec((tm, d), lambda i: (i, 0)),
            pl.BlockSpec((1, d), lambda i: (0, 0)),
            _mod_spec(d, row_of_tile, sc_idx),
            _mod_spec(d, row_of_tile, sh_idx),
        ],
        out_specs=pl.BlockSpec((tm, d), lambda i: (i, 0)),
        out_shape=jax.ShapeDtypeStruct((m, d), BF16),
        compiler_params=_params(1, 32),
        name="norm_mod",
    )(x2d, gain, mod, mod)


def _glu_kernel(h_ref, wa_ref, wg_ref, o_ref):
    h = h_ref[...]
    a = _dot(h, wa_ref[...])
    g = _dot(h, wg_ref[...])
    o_ref[...] = a * _sigmoid(g)


def _glu_proj(h, w_in, conv_w, tm, tn):
    m, d = h.shape
    nj = conv_w // tn
    return pl.pallas_call(
        _glu_kernel,
        grid=(m // tm, nj),
        in_specs=[
            pl.BlockSpec((tm, d), lambda i, j: (i, 0)),
            pl.BlockSpec((d, tn), lambda i, j: (0, j)),
            pl.BlockSpec((d, tn), lambda i, j: (0, nj + j)),
        ],
        out_specs=pl.BlockSpec((tm, tn), lambda i, j: (i, j)),
        out_shape=jax.ShapeDtypeStruct((m, conv_w), F32),
        compiler_params=_params(2, 48),
        name="glu_proj",
    )(h, w_in, w_in)


def _head_proj_kernel(h_ref, w_ref, g_ref, o_ref, *cache_refs, hpt, nb, sb, normalize):
    acc = _dot(h_ref[...], w_ref[...])
    for hh in range(hpt):
        z = acc[:, hh * HEAD_DIM:(hh + 1) * HEAD_DIM]
        if normalize:
            z = z * lax.rsqrt(jnp.mean(z * z, axis=-1, keepdims=True) + EPS) * g_ref[...]
        for b in range(nb):
            blk = z[b * sb:(b + 1) * sb]
            o_ref[b, hh] = blk.astype(o_ref.dtype)
            if cache_refs:
                cache_refs[0][b, 0, hh] = blk


def _head_proj(h, w_in, col0, gain, bsz, seq, n_heads, normalize, want_cache, tm, hpt):
    m, d = h.shape
    tn = hpt * HEAD_DIM
    jb0 = col0 // tn
    if seq >= tm:
        nb, sb, spb = 1, tm, seq // tm
        o_map = lambda i, j: (i // spb, j, i % spb, 0)
        c_map = lambda i, j: (i // spb, 0, j, i % spb, 0)
    else:
        nb, sb = tm // seq, seq
        o_map = lambda i, j: (i, j, 0, 0)
        c_map = lambda i, j: (i, 0, j, 0, 0)
    out_shape = [jax.ShapeDtypeStruct((bsz, n_heads, seq, HEAD_DIM), BF16)]
    out_specs = [pl.BlockSpec((nb, hpt, sb, HEAD_DIM), o_map)]
    if want_cache:
        out_shape.append(jax.ShapeDtypeStruct((bsz, 1, n_heads, seq, HEAD_DIM), F32))
        out_specs.append(pl.BlockSpec((nb, 1, hpt, sb, HEAD_DIM), c_map))
    res = pl.pallas_call(
        functools.partial(_head_proj_kernel, hpt=hpt, nb=nb, sb=sb, normalize=normalize),
        grid=(m // tm, n_heads // hpt),
        in_specs=[
            pl.BlockSpec((tm, d), lambda i, j: (i, 0)),
            pl.BlockSpec((d, tn), lambda i, j: (0, jb0 + j)),
            pl.BlockSpec((1, HEAD_DIM), lambda i, j: (0, 0)),
        ],
        out_specs=out_specs,
        out_shape=out_shape,
        compiler_params=_params(2, 48),
        name="head_proj",
    )(h, w_in, gain)
    return res if want_cache else (res[0], None)


def _conv_kernel(u_ref, up_ref, un_ref, w_ref, b_ref, g_ref, beta_ref, o_ref, pad_ref, y_ref, *, ts, n_s):
    s = pl.program_id(1)
    c = u_ref.shape[-1]
    pad_ref[0:CONV_HALO, :] = jnp.where(s == 0, 0.0, up_ref[...])
    pad_ref[CONV_HALO:CONV_HALO + ts, :] = u_ref[...]
    pad_ref[CONV_HALO + ts:2 * CONV_HALO + ts, :] = jnp.where(s == n_s - 1, 0.0, un_ref[...])
    rc = min(CONV_RC, ts)
    shift = CONV_HALO - CONV_K // 2
    for c0 in range(0, c, LANES):
        wv = w_ref[:, c0:c0 + LANES]
        bv = b_ref[:, c0:c0 + LANES]

        def body(ci, carry, c0=c0, wv=wv, bv=bv):
            r0 = pl.multiple_of(ci * rc, rc)
            chunk = pad_ref[pl.ds(r0, rc + 2 * CONV_HALO), c0:c0 + LANES]
            acc = jnp.zeros((rc, LANES), F32)
            for k in range(CONV_K):
                acc = acc + wv[k:k + 1, :] * chunk[k + shift:k + shift + rc, :]
            y_ref[pl.ds(r0, rc), c0:c0 + LANES] = acc + bv
            return carry

        lax.fori_loop(0, ts // rc, body, 0)
    y = y_ref[...]
    mu = jnp.mean(y, axis=-1, keepdims=True)
    yc = y - mu
    var = jnp.mean(yc * yc, axis=-1, keepdims=True)
    z = yc * lax.rsqrt(var + EPS) * g_ref[...] + beta_ref[...]
    o_ref[...] = (z * _sigmoid(z)).astype(o_ref.dtype)


def _conv_module(u, seq, conv_dw, conv_b, ln_g, ln_b, ts):
    m, c = u.shape
    n_s = seq // ts
    hb = ts // CONV_HALO
    n_hblk = m // CONV_HALO
    main = lambda b, s: (b * n_s + s, 0)
    prev = lambda b, s: (jnp.maximum((b * n_s + s) * hb - 1, 0), 0)
    nxt = lambda b, s: (jnp.minimum((b * n_s + s + 1) * hb, n_hblk - 1), 0)
    vec = lambda rows: pl.BlockSpec((rows, c), lambda b, s: (0, 0))
    return pl.pallas_call(
        functools.partial(_conv_kernel, ts=ts, n_s=n_s),
        grid=(m // seq, n_s),
        in_specs=[
            pl.BlockSpec((ts, c), main),
            pl.BlockSpec((CONV_HALO, c), prev),
            pl.BlockSpec((CONV_HALO, c), nxt),
            vec(CONV_K), vec(1), vec(1), vec(1),
        ],
        out_specs=pl.BlockSpec((ts, c), main),
        out_shape=jax.ShapeDtypeStruct((m, c), BF16),
        scratch_shapes=[
            pltpu.VMEM((ts + 2 * CONV_HALO, c), F32),
            pltpu.VMEM((ts, c), F32),
        ],
        compiler_params=_params(2, 40),
        name="conv_module",
    )(u, u, u, conv_dw, conv_b, ln_g, ln_b)


def _ctx_attn_kernel(q_ref, k_ref, v_ref, o_ref, *, hpt, scale):
    for hh in range(hpt):
        q = q_ref[0, hh]
        k = k_ref[0, hh]
        v = v_ref[0, hh]
        s = _dot_t(q, k) * scale
        p = jnp.exp(s - jnp.max(s, axis=-1, keepdims=True))
        l = jnp.sum(p, axis=-1, keepdims=True)
        o = _dot(p.astype(BF16), v) / l
        o_ref[:, hh * HEAD_DIM:(hh + 1) * HEAD_DIM] = o.astype(o_ref.dtype)


def _ctx_attention(q, k, v, hpt):
    bsz, nh, seq, _ = q.shape
    spec = pl.BlockSpec((1, hpt, seq, HEAD_DIM), lambda b, j: (b, j, 0, 0))
    return pl.pallas_call(
        functools.partial(_ctx_attn_kernel, hpt=hpt, scale=HEAD_DIM ** -0.5),
        grid=(bsz, nh // hpt),
        in_specs=[spec, spec, spec],
        out_specs=pl.BlockSpec((seq, hpt * HEAD_DIM), lambda b, j: (b, j)),
        out_shape=jax.ShapeDtypeStruct((bsz * seq, nh * HEAD_DIM), BF16),
        compiler_params=_params(2, 32),
        name="ctx_attention",
    )(q, k, v)


def _na_kernel(q_ref, k_ref, v_ref, ck_ref, cv_ref, t_ref, o_ref, *, rows, kr, scale):
    ck = ck_ref[0, 0, 0].astype(BF16)
    cv = cv_ref[0, 0, 0].astype(BF16)
    n_lat = kr * GRID_W

    def body(r, carry):
        start = jnp.clip(r - kr // 2, 0, rows - kr)
        q0 = pl.multiple_of(r * GRID_W, GRID_W)
        k0 = pl.multiple_of(start * GRID_W, GRID_W)
        qb = q_ref[0, 0, pl.ds(q0, GRID_W), :]
        kb = k_ref[0, 0, pl.ds(k0, n_lat), :]
        vb = v_ref[0, 0, pl.ds(k0, n_lat), :]
        s_lat = _dot_t(qb, kb) * scale + t_ref[0, r - start]
        s_ctx = _dot_t(qb, ck) * scale
        mx = jnp.maximum(jnp.max(s_lat, axis=-1, keepdims=True), jnp.max(s_ctx, axis=-1, keepdims=True))
        p_lat = jnp.exp(s_lat - mx)
        p_ctx = jnp.exp(s_ctx - mx)
        l = jnp.sum(p_lat, axis=-1, keepdims=True) + jnp.sum(p_ctx, axis=-1, keepdims=True)
        o = (_dot(p_lat.astype(BF16), vb) + _dot(p_ctx.astype(BF16), cv)) / l
        o_ref[pl.ds(q0, GRID_W), :] = o.astype(o_ref.dtype)
        return carry

    lax.fori_loop(0, rows, body, 0)


def _na_bias_table(rpb, kr):
    cols = np.arange(GRID_W)
    c_start = np.clip(cols - WIN_C // 2, 0, GRID_W - WIN_C)
    col_mask = (cols[None, :] >= c_start[:, None]) & (cols[None, :] < c_start[:, None] + WIN_C)
    dc_idx = np.clip(cols[None, :] - cols[:, None], -(WIN_C - 1), WIN_C - 1) + WIN_C - 1
    dr_tab = (WIN_R - 1) - np.arange(kr)[:, None] + np.arange(kr)[None, :]
    bias = rpb[:, dr_tab[:, :, None, None], dc_idx[None, None, :, :]]
    bias = jnp.where(col_mask[None, None, None], bias, MASK_VALUE)
    nh = rpb.shape[0]
    return bias.transpose(0, 1, 3, 2, 4).reshape(nh, kr, GRID_W, kr * GRID_W)


def _na_attention(q, k, v, cache_k, cache_v, rpb):
    bsz, nh, seq, _ = q.shape
    rows = seq // GRID_W
    kr = min(WIN_R, rows)
    past = cache_k.shape[3]
    table = _na_bias_table(rpb, kr)
    qkv = pl.BlockSpec((1, 1, seq, HEAD_DIM), lambda h, b: (b, h, 0, 0))
    cache = pl.BlockSpec((1, 1, 1, past, HEAD_DIM), lambda h, b: (b, 0, h, 0, 0))
    return pl.pallas_call(
        functools.partial(_na_kernel, rows=rows, kr=kr, scale=HEAD_DIM ** -0.5),
        grid=(nh, bsz),
        in_specs=[qkv, qkv, qkv, cache, cache,
                  pl.BlockSpec((1, kr, GRID_W, kr * GRID_W), lambda h, b: (h, 0, 0, 0))],
        out_specs=pl.BlockSpec((seq, HEAD_DIM), lambda h, b: (b, h)),
        out_shape=jax.ShapeDtypeStruct((bsz * seq, nh * HEAD_DIM), BF16),
        compiler_params=_params(2, 32),
        name="na_attention",
    )(q, k, v, cache_k, cache_v, table)


def _out_proj_kernel(uc_ref, att_ref, wt_ref, wb_ref, x_ref, g_ref, o_ref):
    acc = _dot(uc_ref[...], wt_ref[...]) + _dot(att_ref[...], wb_ref[...])
    o_ref[...] = x_ref[...] + g_ref[0] * acc


def _out_proj(uc, att, w_out, x2d, mod, row_of_tile, gate_idx, tm, tn):
    m, d = x2d.shape
    cw = uc.shape[1]
    aw = att.shape[1]
    assert cw == aw
    return pl.pallas_call(
        _out_proj_kernel,
        grid=(m // tm, d // tn),
        in_specs=[
            pl.BlockSpec((tm, cw), lambda i, j: (i, 0)),
            pl.BlockSpec((tm, aw), lambda i, j: (i, 0)),
            pl.BlockSpec((cw, tn), lambda i, j: (0, j)),
            pl.BlockSpec((aw, tn), lambda i, j: (1, j)),
            pl.BlockSpec((tm, tn), lambda i, j: (i, j)),
            pl.BlockSpec((1, 1, tn), lambda i, j: (row_of_tile(i) * N_MOD + gate_idx, 0, j)),
        ],
        out_specs=pl.BlockSpec((tm, tn), lambda i, j: (i, j)),
        out_shape=jax.ShapeDtypeStruct((m, d), F32),
        compiler_params=_params(2, 48),
        name="out_proj",
    )(uc, att, w_out, w_out, x2d, mod)


def _norm_route_kernel(y_ref, g_ref, sc_ref, sh_ref, wr_ref, br_ref, h_ref, r_ref):
    h = _rms_mod(y_ref[...], g_ref[...], sc_ref[0], sh_ref[0])
    h_ref[...] = h
    logits = _dot(h.astype(BF16), wr_ref[...]) + br_ref[...]
    lane = lax.broadcasted_iota(jnp.int32, logits.shape, 1)
    lane_f = lane.astype(F32)
    no_lane = float(ROUTE_COLS)

    def first_argmax(mask):
        vals = jnp.where(mask, logits, MASK_VALUE)
        top = jnp.max(vals, axis=-1, keepdims=True)
        idx = jnp.min(jnp.where(mask & (logits == top), lane_f, no_lane), axis=-1, keepdims=True)
        return vals, top, idx

    g_mask = lane < N_GROUPS
    g_vals, g_top, g_idx = first_argmax(g_mask)
    g_prob = 1.0 / jnp.sum(jnp.where(g_mask, jnp.exp(g_vals - g_top), 0.0), axis=-1, keepdims=True)
    lo = N_GROUPS + EXPERTS_PER_GROUP * g_idx
    e_mask = (lane_f >= lo) & (lane_f < lo + EXPERTS_PER_GROUP)
    _, top1, idx1 = first_argmax(e_mask)
    _, top2, idx2 = first_argmax(e_mask & (lane_f != idx1))
    t = jnp.exp(top2 - top1)
    c1 = g_prob / (1.0 + t)
    c2 = g_prob * t / (1.0 + t)
    e1 = idx1 - N_GROUPS
    e2 = idx2 - N_GROUPS
    r_ref[...] = jnp.where(lane == 0, e1, jnp.where(lane == 1, e2, jnp.where(lane == 2, c1,
                           jnp.where(lane == 3, c2, 0.0))))


def _norm_route(y2d, gain, mod, row_of_tile, sc_idx, sh_idx, w_route, b_route, tm):
    m, d = y2d.shape
    return pl.pallas_call(
        _norm_route_kernel,
        grid=(m // tm,),
        in_specs=[
            pl.BlockSpec((tm, d), lambda i: (i, 0)),
            pl.BlockSpec((1, d), lambda i: (0, 0)),
            _mod_spec(d, row_of_tile, sc_idx),
            _mod_spec(d, row_of_tile, sh_idx),
            pl.BlockSpec((d, ROUTE_COLS), lambda i: (0, 0)),
            pl.BlockSpec((1, ROUTE_COLS), lambda i: (0, 0)),
        ],
        out_specs=[
            pl.BlockSpec((tm, d), lambda i: (i, 0)),
            pl.BlockSpec((tm, ROUTE_COLS), lambda i: (i, 0)),
        ],
        out_shape=[
            jax.ShapeDtypeStruct((m, d), F32),
            jax.ShapeDtypeStruct((m, ROUTE_COLS), F32),
        ],
        compiler_params=_params(1, 40),
        name="norm_route",
    )(y2d, gain, mod, mod, w_route, b_route)


def _row_copy(src_hbm, src_row, dst, dst_row, sem):
    return pltpu.make_async_copy(src_hbm.at[pl.ds(src_row, 1)], dst.at[pl.ds(dst_row, 1)], sem)


def _moe_kernel(te_ref, nu_ref, src_ref, h_hbm, roww_ref, wg_ref, wu_ref, wd_ref, o_ref, buf, sem, *, tm):
    del te_ref

    @pl.when(pl.program_id(0) < nu_ref[0])
    def _():
        def issue(r, carry):
            _row_copy(h_hbm, src_ref[0, 0, r], buf, r, sem).start()
            return carry

        lax.fori_loop(0, tm, issue, 0)

        def wait(r, carry):
            _row_copy(h_hbm, 0, buf, r, sem).wait()
            return carry

        lax.fori_loop(0, tm, wait, 0)
        x = buf[...].astype(BF16)
        a = _dot(x, wg_ref[0])
        u = _dot(x, wu_ref[0])
        hid = (a * _sigmoid(a)) * u * roww_ref[...]
        o_ref[...] = _dot(hid.astype(BF16), wd_ref[0])

    @pl.when(pl.program_id(0) >= nu_ref[0])
    def _():
        o_ref[...] = jnp.zeros_like(o_ref)


def _moe_experts(h, tile_expert, n_used, row_src, row_w, w_gate, w_up, w_down, tm):
    m, d = h.shape
    n_tiles = row_src.shape[0]
    f = w_gate.shape[-1]
    used = lambda i, te, nu: jnp.minimum(i, nu[0] - 1)
    grid_spec = pltpu.PrefetchScalarGridSpec(
        num_scalar_prefetch=2,
        grid=(n_tiles,),
        in_specs=[
            pl.BlockSpec((1, 1, tm), lambda i, te, nu: (used(i, te, nu), 0, 0), memory_space=pltpu.SMEM),
            pl.BlockSpec(memory_space=pl.ANY),
            pl.BlockSpec((tm, 1), lambda i, te, nu: (used(i, te, nu), 0)),
            pl.BlockSpec((1, d, f), lambda i, te, nu: (te[used(i, te, nu)], 0, 0)),
            pl.BlockSpec((1, d, f), lambda i, te, nu: (te[used(i, te, nu)], 0, 0)),
            pl.BlockSpec((1, f, d), lambda i, te, nu: (te[used(i, te, nu)], 0, 0)),
        ],
        out_specs=pl.BlockSpec((tm, d), lambda i, te, nu: (i, 0)),
        scratch_shapes=[pltpu.VMEM((tm, d), F32), pltpu.SemaphoreType.DMA(())],
    )
    return pl.pallas_call(
        functools.partial(_moe_kernel, tm=tm),
        grid_spec=grid_spec,
        out_shape=jax.ShapeDtypeStruct((n_tiles * tm, d), F32),
        compiler_params=_params(1, 56),
        name="moe_experts",
    )(tile_expert, n_used, row_src, h, row_w, w_gate, w_up, w_down)


def _combine_kernel(pos_ref, ys_hbm, y_ref, g_ref, o_ref, buf, sem, *, tm):
    def issue(r, carry):
        _row_copy(ys_hbm, pos_ref[0, 0, 2 * r], buf.at[0], r, sem).start()
        _row_copy(ys_hbm, pos_ref[0, 0, 2 * r + 1], buf.at[1], r, sem).start()
        return carry

    lax.fori_loop(0, tm, issue, 0)

    def wait(r, carry):
        _row_copy(ys_hbm, 0, buf.at[0], r, sem).wait()
        _row_copy(ys_hbm, 0, buf.at[1], r, sem).wait()
        return carry

    lax.fori_loop(0, tm, wait, 0)
    o_ref[...] = y_ref[...] + g_ref[0] * (buf[0] + buf[1])


def _moe_combine(ys, pos, y2d, mod, row_of_tile, gate_idx, tm):
    m, d = y2d.shape
    return pl.pallas_call(
        functools.partial(_combine_kernel, tm=tm),
        grid=(m // tm,),
        in_specs=[
            pl.BlockSpec((1, 1, 2 * tm), lambda i: (i, 0, 0), memory_space=pltpu.SMEM),
            pl.BlockSpec(memory_space=pl.ANY),
            pl.BlockSpec((tm, d), lambda i: (i, 0)),
            _mod_spec(d, row_of_tile, gate_idx),
        ],
        out_specs=pl.BlockSpec((tm, d), lambda i: (i, 0)),
        out_shape=jax.ShapeDtypeStruct((m, d), F32),
        scratch_shapes=[pltpu.VMEM((2, tm, d), F32), pltpu.SemaphoreType.DMA(())],
        compiler_params=_params(1, 40),
        name="moe_combine",
    )(pos, ys, y2d, mod)


def _routing_plan(route, tm):
    n_tok = route.shape[0]
    n_pair = 2 * n_tok
    n_tiles = n_pair // tm + N_EXPERTS
    expert = route[:, 0:2].astype(jnp.int32).reshape(n_pair)
    weight = route[:, 2:4].reshape(n_pair)
    onehot = (expert[:, None] == jnp.arange(N_EXPERTS, dtype=jnp.int32)[None, :]).astype(jnp.int32)
    csum = jnp.cumsum(onehot, axis=0)
    counts = csum[-1]
    rank = jnp.sum((csum - 1) * onehot, axis=1)
    padded = ((counts + tm - 1) // tm) * tm
    ends = jnp.cumsum(padded)
    starts = ends - padded
    pos = jnp.sum(starts[None, :] * onehot, axis=1) + rank
    token = jnp.arange(n_pair, dtype=jnp.int32) // 2
    row_src = jnp.zeros((n_tiles * tm,), jnp.int32).at[pos].set(token)
    row_w = jnp.zeros((n_tiles * tm,), F32).at[pos].set(weight)
    tile_start = jnp.arange(n_tiles, dtype=jnp.int32) * tm
    tile_expert = jnp.minimum(jnp.sum((tile_start[:, None] >= ends[None, :]).astype(jnp.int32), axis=1),
                              N_EXPERTS - 1)
    n_used = (ends[-1] // tm).astype(jnp.int32).reshape(1)
    return tile_expert, n_used, row_src.reshape(n_tiles, 1, tm), row_w.reshape(n_tiles * tm, 1), pos


def _trunk(x, row_of_tokens, mod, attend, p):
    bsz, seq, d = x.shape
    m = bsz * seq
    x2d = x.reshape(m, d)
    conv_w = p["conv_dw"].shape[-1]
    att_w = (p["w_in"].shape[1] - 2 * conv_w) // 3
    n_heads = att_w // HEAD_DIM
    hpt = min(PROJ_HEADS, n_heads)
    tn = hpt * HEAD_DIM
    ptm = min(PROJ_TM, m)
    ntm = min(NORM_TM, m)

    def rows_of(tm):
        return lambda i: row_of_tokens(i * tm)

    h1 = _norm_mod(x2d, p["norm1_g"], mod, rows_of(ntm), 1, 0, ntm)
    u = _glu_proj(h1, p["w_in"], conv_w, ptm, min(tn, conv_w))
    q, _ = _head_proj(h1, p["w_in"], 2 * conv_w, p["q_norm_g"], bsz, seq, n_heads, True, False, ptm, hpt)
    k, k_cache = _head_proj(h1, p["w_in"], 2 * conv_w + att_w, p["k_norm_g"], bsz, seq, n_heads, True,
                            attend is None, ptm, hpt)
    v, v_cache = _head_proj(h1, p["w_in"], 2 * conv_w + 2 * att_w, p["k_norm_g"], bsz, seq, n_heads, False,
                            attend is None, ptm, hpt)
    uc = _conv_module(u, seq, p["conv_dw"], p["conv_b"], p["conv_ln_g"], p["conv_ln_b"], min(CONV_TS, seq))
    if attend is None:
        att = _ctx_attention(q, k, v, hpt)
    else:
        att = attend(q, k, v)
    y = _out_proj(uc, att, p["w_out"], x2d, mod, rows_of(ptm), 2, ptm, min(tn, d))
    h2, route = _norm_route(y, p["norm2_g"], mod, rows_of(ntm), 4, 3, p["w_route"], p["b_route"], ntm)
    mtm = min(MOE_TM, m)
    tile_expert, n_used, row_src, row_w, pos = _routing_plan(route, mtm)
    ys = _moe_experts(h2, tile_expert, n_used, row_src, row_w, p["w_gate"], p["w_up"], p["w_down"], mtm)
    ctm = min(COMB_TM, m)
    out = _moe_combine(ys, pos.reshape(m // ctm, 1, 2 * ctm), y, mod, rows_of(ctm), 5, ctm)
    return out.reshape(bsz, seq, d), k_cache, v_cache


def kernel(x_prompt, x_sample, cache_k, cache_v, c, c_ctx, norm1_g, w_mod, b_mod, w_in, conv_dw, conv_b,
           conv_ln_g, conv_ln_b, q_norm_g, k_norm_g, rpb, w_out, norm2_g, w_rg, b_rg, w_re, b_re, w_gate,
           w_up, w_down):
    depth = w_in.shape[0]
    assert depth == 1, "one trunk layer per step"
    d = x_prompt.shape[-1]
    dec_b, dec_s = x_sample.shape[0], x_sample.shape[1]
    assert dec_b < COND_ROWS
    ctx_row = dec_b

    cond = jnp.zeros((COND_ROWS, d), F32).at[:dec_b].set(c).at[ctx_row].set(c_ctx)
    mod = _modulation(cond, w_mod[0], b_mod[0][None, :]).reshape(COND_ROWS * N_MOD, 1, d)

    pad_cols = ROUTE_COLS - N_GROUPS - N_EXPERTS
    w_route = jnp.concatenate([w_rg[0], w_re[0], jnp.zeros((d, pad_cols), F32)], axis=1).astype(BF16)
    b_route = jnp.concatenate([b_rg[0], b_re[0], jnp.zeros((pad_cols,), F32)])[None, :]
    p = dict(
        norm1_g=norm1_g[0][None, :], norm2_g=norm2_g[0][None, :],
        w_in=w_in[0].astype(BF16), w_out=w_out[0].astype(BF16),
        conv_dw=conv_dw[0], conv_b=conv_b[0][None, :],
        conv_ln_g=conv_ln_g[0][None, :], conv_ln_b=conv_ln_b[0][None, :],
        q_norm_g=q_norm_g[0][None, :], k_norm_g=k_norm_g[0][None, :],
        w_route=w_route, b_route=b_route,
        w_gate=w_gate[0].astype(BF16), w_up=w_up[0].astype(BF16), w_down=w_down[0].astype(BF16),
    )

    y_prompt, k_new, v_new = _trunk(x_prompt, lambda tok: ctx_row, mod, None, p)
    attend = functools.partial(_na_attention, cache_k=cache_k, cache_v=cache_v, rpb=rpb[0])
    y_sample, _, _ = _trunk(x_sample, lambda tok: tok // dec_s, mod, attend, p)
    return (y_prompt, y_sample, k_new, v_new)
```

```python
import functools

import numpy as np
import jax
import jax.numpy as jnp
from jax import lax
from jax.experimental import pallas as pl
from jax.experimental.pallas import tpu as pltpu

F32 = jnp.float32
BF16 = jnp.bfloat16

GRID_W = 64
WIN_R = 8
WIN_C = 16
CONV_K = 31
HEAD_DIM = 128
N_GROUPS = 4
EXPERTS_PER_GROUP = 8
N_EXPERTS = N_GROUPS * EXPERTS_PER_GROUP
N_MOD = 6
EPS = 1e-6
MASK_VALUE = -1e30

LANES = 128
SUBLANES = 8
MIB = 1024 * 1024

COND_ROWS = 16
MOD_TN = 512
NORM_TM = 256
PROJ_TM = 1024
PROJ_HEADS = 4
CONV_TS = 256
CONV_HALO = 16
CONV_RC = 128
NA_RB = 4
NA_UNROLL = 4
MOE_TM = 256
COMB_TM = 256
ROUTE_COLS = LANES


def _params(n_axes, vmem_mib):
    return pltpu.CompilerParams(
        dimension_semantics=("arbitrary",) * n_axes,
        vmem_limit_bytes=vmem_mib * MIB,
    )


def _sigmoid(x):
    return 1.0 / (1.0 + jnp.exp(-x))


def _dot(a, b):
    return jnp.dot(a, b, preferred_element_type=F32)


def _dot_t(a, b):
    return lax.dot_general(a, b, (((1,), (1,)), ((), ())), preferred_element_type=F32)


def _mod_kernel(c_ref, w_ref, b_ref, o_ref):
    c = c_ref[...]
    s = (c * _sigmoid(c)).astype(BF16)
    o_ref[...] = _dot(s, w_ref[...].astype(BF16)) + b_ref[...]


def _modulation(cond, w_mod, b_mod):
    d, n = w_mod.shape
    tn = min(MOD_TN, n)
    return pl.pallas_call(
        _mod_kernel,
        grid=(n // tn,),
        in_specs=[
            pl.BlockSpec((COND_ROWS, d), lambda j: (0, 0)),
            pl.BlockSpec((d, tn), lambda j: (0, j)),
            pl.BlockSpec((1, tn), lambda j: (0, j)),
        ],
        out_specs=pl.BlockSpec((COND_ROWS, tn), lambda j: (0, j)),
        out_shape=jax.ShapeDtypeStruct((COND_ROWS, n), F32),
        compiler_params=_params(1, 40),
        name="modulation",
    )(cond, w_mod, b_mod)


def _rms_mod(x, g, sc, sh):
    y = x * lax.rsqrt(jnp.mean(x * x, axis=-1, keepdims=True) + EPS) * g
    return y * (1.0 + sc) + sh


def _norm_mod_kernel(x_ref, g_ref, sc_ref, sh_ref, o_ref):
    o_ref[...] = _rms_mod(x_ref[...], g_ref[...], sc_ref[0], sh_ref[0]).astype(o_ref.dtype)


def _mod_spec(d, row_of_tile, which):
    return pl.BlockSpec((1, 1, d), lambda i, *_: (row_of_tile(i) * N_MOD + which, 0, 0))


def _norm_mod(x2d, gain, mod, row_of_tile, sc_idx, sh_idx, tm):
    m, d = x2d.shape
    return pl.pallas_call(
        _norm_mod_kernel,
        grid=(m // tm,),
        in_specs=[
            pl.BlockSpec((tm, d), lambda i: (i, 0)),
            pl.BlockSpec((1, d), lambda i: (0, 0)),
            _mod_spec(d, row_of_tile, sc_idx),
            _mod_spec(d, row_of_tile, sh_idx),
        ],
        out_specs=pl.BlockSpec((tm, d), lambda i: (i, 0)),
        out_shape=jax.ShapeDtypeStruct((m, d), BF16),
        compiler_params=_params(1, 32),
        name="norm_mod",
    )(x2d, gain, mod, mod)


def _glu_kernel(h_ref, wa_ref, wg_ref, o_ref):
    h = h_ref[...]
    a = _dot(h, wa_ref[...])
    g = _dot(h, wg_ref[...])
    o_ref[...] = a * _sigmoid(g)


def _glu_proj(h, w_in, conv_w, tm, tn):
    m, d = h.shape
    nj = conv_w // tn
    return pl.pallas_call(
        _glu_kernel,
        grid=(m // tm, nj),
        in_specs=[
            pl.BlockSpec((tm, d), lambda i, j: (i, 0)),
            pl.BlockSpec((d, tn), lambda i, j: (0, j)),
            pl.BlockSpec((d, tn), lambda i, j: (0, nj + j)),
        ],
        out_specs=pl.BlockSpec((tm, tn), lambda i, j: (i, j)),
        out_shape=jax.ShapeDtypeStruct((m, conv_w), F32),
        compiler_params=_params(2, 48),
        name="glu_proj",
    )(h, w_in, w_in)


def _head_proj_kernel(h_ref, w_ref, *refs, hpt, nb, sb, normalize):
    if normalize:
        g_ref, o_ref, *cache_refs = refs
    else:
        o_ref, *cache_refs = refs
    acc = _dot(h_ref[...], w_ref[...])
    for hh in range(hpt):
        z = acc[:, hh * HEAD_DIM:(hh + 1) * HEAD_DIM]
        if normalize:
            z = z * lax.rsqrt(jnp.mean(z * z, axis=-1, keepdims=True) + EPS) * g_ref[...]
        for b in range(nb):
            blk = z[b * sb:(b + 1) * sb]
            o_ref[b, hh] = blk.astype(o_ref.dtype)
            if cache_refs:
                cache_refs[0][b, 0, hh] = blk


def _head_proj(h, w_in, col0, gain, bsz, seq, n_heads, want_cache, tm, hpt):
    m, d = h.shape
    tn = hpt * HEAD_DIM
    jb0 = col0 // tn
    if seq >= tm:
        nb, sb, spb = 1, tm, seq // tm
        o_map = lambda i, j: (i // spb, j, i % spb, 0)
        c_map = lambda i, j: (i // spb, 0, j, i % spb, 0)
    else:
        nb, sb = tm // seq, seq
        o_map = lambda i, j: (i, j, 0, 0)
        c_map = lambda i, j: (i, 0, j, 0, 0)
    out_shape = [jax.ShapeDtypeStruct((bsz, n_heads, seq, HEAD_DIM), BF16)]
    out_specs = [pl.BlockSpec((nb, hpt, sb, HEAD_DIM), o_map)]
    if want_cache:
        out_shape.append(jax.ShapeDtypeStruct((bsz, 1, n_heads, seq, HEAD_DIM), F32))
        out_specs.append(pl.BlockSpec((nb, 1, hpt, sb, HEAD_DIM), c_map))
    normalize = gain is not None
    in_specs = [
        pl.BlockSpec((tm, d), lambda i, j: (i, 0)),
        pl.BlockSpec((d, tn), lambda i, j: (0, jb0 + j)),
    ]
    operands = [h, w_in]
    if normalize:
        in_specs.append(pl.BlockSpec((1, HEAD_DIM), lambda i, j: (0, 0)))
        operands.append(gain)
    res = pl.pallas_call(
        functools.partial(_head_proj_kernel, hpt=hpt, nb=nb, sb=sb, normalize=normalize),
        grid=(m // tm, n_heads // hpt),
        in_specs=in_specs,
        out_specs=out_specs,
        out_shape=out_shape,
        compiler_params=_params(2, 48),
        name="head_proj",
    )(*operands)
    return res if want_cache else (res[0], None)


def _conv_kernel(u_ref, up_ref, un_ref, w_ref, b_ref, g_ref, beta_ref, o_ref, pad_ref, y_ref, *, ts, n_s):
    s = pl.program_id(1)
    c = u_ref.shape[-1]
    pad_ref[0:CONV_HALO, :] = jnp.where(s == 0, 0.0, up_ref[...])
    pad_ref[CONV_HALO:CONV_HALO + ts, :] = u_ref[...]
    pad_ref[CONV_HALO + ts:2 * CONV_HALO + ts, :] = jnp.where(s == n_s - 1, 0.0, un_ref[...])
    rc = min(CONV_RC, ts)
    shift = CONV_HALO - CONV_K // 2
    for c0 in range(0, c, LANES):
        wv = w_ref[:, c0:c0 + LANES]
        bv = b_ref[:, c0:c0 + LANES]

        def body(ci, carry, c0=c0, wv=wv, bv=bv):
            r0 = pl.multiple_of(ci * rc, rc)
            chunk = pad_ref[pl.ds(r0, rc + 2 * CONV_HALO), c0:c0 + LANES]
            acc = jnp.zeros((rc, LANES), F32)
            for j in range(SUBLANES):
                tiles = [(k, (k + shift) // SUBLANES) for k in range(CONV_K) if (k + shift) % SUBLANES == j]
                if not tiles:
                    continue
                shifted = chunk if j == 0 else pltpu.roll(chunk, chunk.shape[0] - j, axis=0)
                for k, a in tiles:
                    acc = acc + wv[k:k + 1, :] * shifted[a * SUBLANES:a * SUBLANES + rc, :]
            y_ref[pl.ds(r0, rc), c0:c0 + LANES] = acc + bv
            return carry

        lax.fori_loop(0, ts // rc, body, 0)
    y = y_ref[...]
    mu = jnp.mean(y, axis=-1, keepdims=True)
    yc = y - mu
    var = jnp.mean(yc * yc, axis=-1, keepdims=True)
    z = yc * lax.rsqrt(var + EPS) * g_ref[...] + beta_ref[...]
    o_ref[...] = (z * _sigmoid(z)).astype(o_ref.dtype)


def _conv_module(u, seq, conv_dw, conv_b, ln_g, ln_b, ts):
    m, c = u.shape
    n_s = seq // ts
    hb = ts // CONV_HALO
    n_hblk = m // CONV_HALO
    main = lambda b, s: (b * n_s + s, 0)
    prev = lambda b, s: (jnp.maximum((b * n_s + s) * hb - 1, 0), 0)
    nxt = lambda b, s: (jnp.minimum((b * n_s + s + 1) * hb, n_hblk - 1), 0)
    vec = lambda rows: pl.BlockSpec((rows, c), lambda b, s: (0, 0))
    return pl.pallas_call(
        functools.partial(_conv_kernel, ts=ts, n_s=n_s),
        grid=(m // seq, n_s),
        in_specs=[
            pl.BlockSpec((ts, c), main),
            pl.BlockSpec((CONV_HALO, c), prev),
            pl.BlockSpec((CONV_HALO, c), nxt),
            vec(CONV_K), vec(1), vec(1), vec(1),
        ],
        out_specs=pl.BlockSpec((ts, c), main),
        out_shape=jax.ShapeDtypeStruct((m, c), BF16),
        scratch_shapes=[
            pltpu.VMEM((ts + 2 * CONV_HALO, c), F32),
            pltpu.VMEM((ts, c), F32),
        ],
        compiler_params=_params(2, 40),
        name="conv_module",
    )(u, u, u, conv_dw, conv_b, ln_g, ln_b)


def _ctx_attn_kernel(q_ref, k_ref, v_ref, o_ref, *, hpt, scale):
    for hh in range(hpt):
        q = q_ref[0, hh]
        k = k_ref[0, hh]
        v = v_ref[0, hh]
        s = _dot_t(q, k) * scale
        p = jnp.exp(s - jnp.max(s, axis=-1, keepdims=True))
        l = jnp.sum(p, axis=-1, keepdims=True)
        o = _dot(p.astype(BF16), v) / l
        o_ref[:, hh * HEAD_DIM:(hh + 1) * HEAD_DIM] = o.astype(o_ref.dtype)


def _ctx_attention(q, k, v, hpt):
    bsz, nh, seq, _ = q.shape
    spec = pl.BlockSpec((1, hpt, seq, HEAD_DIM), lambda b, j: (b, j, 0, 0))
    return pl.pallas_call(
        functools.partial(_ctx_attn_kernel, hpt=hpt, scale=HEAD_DIM ** -0.5),
        grid=(bsz, nh // hpt),
        in_specs=[spec, spec, spec],
        out_specs=pl.BlockSpec((seq, hpt * HEAD_DIM), lambda b, j: (b, j)),
        out_shape=jax.ShapeDtypeStruct((bsz * seq, nh * HEAD_DIM), BF16),
        compiler_params=_params(2, 32),
        name="ctx_attention",
    )(q, k, v)


def _na_block_plan(rows, kr):
    slab = kr + NA_RB - 1
    n_blk = rows // NA_RB
    assert rows % NA_RB == 0 and rows >= slab and n_blk >= 3
    plans = []
    for b in range(n_blk):
        ks = min(max(b * NA_RB - kr // 2, 0), rows - slab)
        plan = np.full((NA_RB, slab), -1, np.int64)
        for rho in range(NA_RB):
            r = b * NA_RB + rho
            start = min(max(r - kr // 2, 0), rows - kr)
            for i in range(slab):
                j = ks + i
                if start <= j < start + kr:
                    plan[rho, i] = j - r + WIN_R - 1
        plans.append(plan)
    for b in range(2, n_blk - 1):
        assert (plans[b] == plans[1]).all()
    return np.stack([plans[0], plans[1], plans[-1]])


def _na_kernel(q_ref, k_ref, v_ref, ck_ref, cv_ref, t_ref, o_ref, *, rows, kr, scale):
    ck = ck_ref[0, 0, 0].astype(BF16)
    cv = cv_ref[0, 0, 0].astype(BF16)
    slab = kr + NA_RB - 1
    n_blk = rows // NA_RB
    nq = NA_RB * GRID_W
    nk = slab * GRID_W

    def body(b, carry):
        ks = jnp.clip(b * NA_RB - kr // 2, 0, rows - slab)
        kind = jnp.where(b == 0, 0, jnp.where(b == n_blk - 1, 2, 1))
        q0 = pl.multiple_of(b * nq, nq)
        k0 = pl.multiple_of(ks * GRID_W, GRID_W)
        qb = q_ref[0, 0, pl.ds(q0, nq), :]
        kb = k_ref[0, 0, pl.ds(k0, nk), :]
        vb = v_ref[0, 0, pl.ds(k0, nk), :]
        s_lat = _dot_t(qb, kb) * scale + t_ref[0, kind]
        s_ctx = _dot_t(qb, ck) * scale
        mx = jnp.maximum(jnp.max(s_lat, axis=-1, keepdims=True), jnp.max(s_ctx, axis=-1, keepdims=True))
        p_lat = jnp.exp(s_lat - mx)
        p_ctx = jnp.exp(s_ctx - mx)
        l = jnp.sum(p_lat, axis=-1, keepdims=True) + jnp.sum(p_ctx, axis=-1, keepdims=True)
        o = (_dot(p_lat.astype(BF16), vb) + _dot(p_ctx.astype(BF16), cv)) / l
        o_ref[pl.ds(q0, nq), :] = o.astype(o_ref.dtype)
        return carry

    lax.fori_loop(0, n_blk, body, 0, unroll=NA_UNROLL)


def _na_bias_table(rpb, rows, kr):
    plan = _na_block_plan(rows, kr)
    slab = plan.shape[-1]
    cols = np.arange(GRID_W)
    c_start = np.clip(cols - WIN_C // 2, 0, GRID_W - WIN_C)
    col_mask = (cols[None, :] >= c_start[:, None]) & (cols[None, :] < c_start[:, None] + WIN_C)
    dc_idx = np.clip(cols[None, :] - cols[:, None], -(WIN_C - 1), WIN_C - 1) + WIN_C - 1
    row_sel = (plan[..., None] == np.arange(2 * WIN_R - 1)).astype(np.float32)
    col_sel = (dc_idx[..., None] == np.arange(2 * WIN_C - 1)).astype(np.float32)
    bias = jnp.einsum("vpid,hdc,qkc->hvpqik", row_sel, rpb, col_sel, precision=lax.Precision.HIGHEST)
    valid = (plan >= 0)[None, :, :, None, :, None] & col_mask[None, None, None, :, None, :]
    bias = jnp.where(valid, bias, MASK_VALUE)
    return bias.reshape(rpb.shape[0], 3, NA_RB * GRID_W, slab * GRID_W)


def _na_attention(q, k, v, cache_k, cache_v, rpb):
    bsz, nh, seq, _ = q.shape
    rows = seq // GRID_W
    kr = min(WIN_R, rows)
    past = cache_k.shape[3]
    table = _na_bias_table(rpb, rows, kr)
    qkv = pl.BlockSpec((1, 1, seq, HEAD_DIM), lambda h, b: (b, h, 0, 0))
    cache = pl.BlockSpec((1, 1, 1, past, HEAD_DIM), lambda h, b: (b, 0, h, 0, 0))
    return pl.pallas_call(
        functools.partial(_na_kernel, rows=rows, kr=kr, scale=HEAD_DIM ** -0.5),
        grid=(nh, bsz),
        in_specs=[qkv, qkv, qkv, cache, cache,
                  pl.BlockSpec((1,) + table.shape[1:], lambda h, b: (h, 0, 0, 0))],
        out_specs=pl.BlockSpec((seq, HEAD_DIM), lambda h, b: (b, h)),
        out_shape=jax.ShapeDtypeStruct((bsz * seq, nh * HEAD_DIM), BF16),
        compiler_params=_params(2, 32),
        name="na_attention",
    )(q, k, v, cache_k, cache_v, table)


def _out_proj_kernel(uc_ref, att_ref, wt_ref, wb_ref, x_ref, g_ref, o_ref):
    acc = _dot(uc_ref[...], wt_ref[...]) + _dot(att_ref[...], wb_ref[...])
    o_ref[...] = x_ref[...] + g_ref[0] * acc


def _out_proj(uc, att, w_out, x2d, mod, row_of_tile, gate_idx, tm, tn):
    m, d = x2d.shape
    cw = uc.shape[1]
    aw = att.shape[1]
    assert cw == aw
    return pl.pallas_call(
        _out_proj_kernel,
        grid=(m // tm, d // tn),
        in_specs=[
            pl.BlockSpec((tm, cw), lambda i, j: (i, 0)),
            pl.BlockSpec((tm, aw), lambda i, j: (i, 0)),
            pl.BlockSpec((cw, tn), lambda i, j: (0, j)),
            pl.BlockSpec((aw, tn), lambda i, j: (1, j)),
            pl.BlockSpec((tm, tn), lambda i, j: (i, j)),
            pl.BlockSpec((1, 1, tn), lambda i, j: (row_of_tile(i) * N_MOD + gate_idx, 0, j)),
        ],
        out_specs=pl.BlockSpec((tm, tn), lambda i, j: (i, j)),
        out_shape=jax.ShapeDtypeStruct((m, d), F32),
        compiler_params=_params(2, 48),
        name="out_proj",
    )(uc, att, w_out, w_out, x2d, mod)


def _pack_bf16_pairs(x):
    half = x.shape[-1] // 2
    lo = lax.bitcast_convert_type(x[:, :half].astype(BF16).astype(F32), jnp.uint32)
    hi = lax.bitcast_convert_type(x[:, half:].astype(BF16).astype(F32), jnp.uint32)
    return (lo >> 16) | (hi & jnp.uint32(0xFFFF0000))


def _unpack_bf16_pairs(w):
    lo = lax.bitcast_convert_type(w << 16, F32)
    hi = lax.bitcast_convert_type(w & jnp.uint32(0xFFFF0000), F32)
    return lo, hi


def _norm_route_kernel(ya_ref, yb_ref, g_ref, sc_ref, sh_ref, wr_ref, br_ref, h_ref, r_ref, *, n_a):
    i = pl.program_id(0)

    @pl.when(i < n_a)
    def _():
        _norm_route_tile(ya_ref, g_ref, sc_ref, sh_ref, wr_ref, br_ref, h_ref, r_ref)

    @pl.when(i >= n_a)
    def _():
        _norm_route_tile(yb_ref, g_ref, sc_ref, sh_ref, wr_ref, br_ref, h_ref, r_ref)


def _norm_route_tile(y_ref, g_ref, sc_ref, sh_ref, wr_ref, br_ref, h_ref, r_ref):
    h = _rms_mod(y_ref[...], g_ref[...], sc_ref[0], sh_ref[0])
    h_ref[...] = _pack_bf16_pairs(h)
    logits = _dot(h.astype(BF16), wr_ref[...]) + br_ref[...]
    lane = lax.broadcasted_iota(jnp.int32, logits.shape, 1)
    lane_f = lane.astype(F32)
    no_lane = float(ROUTE_COLS)

    def first_argmax(mask):
        vals = jnp.where(mask, logits, MASK_VALUE)
        top = jnp.max(vals, axis=-1, keepdims=True)
        idx = jnp.min(jnp.where(mask & (logits == top), lane_f, no_lane), axis=-1, keepdims=True)
        return vals, top, idx

    g_mask = lane < N_GROUPS
    g_vals, g_top, g_idx = first_argmax(g_mask)
    g_prob = 1.0 / jnp.sum(jnp.where(g_mask, jnp.exp(g_vals - g_top), 0.0), axis=-1, keepdims=True)
    lo = N_GROUPS + EXPERTS_PER_GROUP * g_idx
    e_mask = (lane_f >= lo) & (lane_f < lo + EXPERTS_PER_GROUP)
    _, top1, idx1 = first_argmax(e_mask)
    _, top2, idx2 = first_argmax(e_mask & (lane_f != idx1))
    t = jnp.exp(top2 - top1)
    c1 = g_prob / (1.0 + t)
    c2 = g_prob * t / (1.0 + t)
    e1 = idx1 - N_GROUPS
    e2 = idx2 - N_GROUPS
    r_ref[...] = jnp.where(lane == 0, e1, jnp.where(lane == 1, e2, jnp.where(lane == 2, c1,
                           jnp.where(lane == 3, c2, 0.0))))


def _norm_route(ya, yb, gain, mod, row_of_tile, sc_idx, sh_idx, w_route, b_route, tm):
    d = ya.shape[1]
    n_a = ya.shape[0] // tm
    m = ya.shape[0] + yb.shape[0]
    return pl.pallas_call(
        functools.partial(_norm_route_kernel, n_a=n_a),
        grid=(m // tm,),
        in_specs=[
            pl.BlockSpec((tm, d), lambda i: (jnp.minimum(i, n_a - 1), 0)),
            pl.BlockSpec((tm, d), lambda i: (jnp.maximum(i - n_a, 0), 0)),
            pl.BlockSpec((1, d), lambda i: (0, 0)),
            _mod_spec(d, row_of_tile, sc_idx),
            _mod_spec(d, row_of_tile, sh_idx),
            pl.BlockSpec((d, ROUTE_COLS), lambda i: (0, 0)),
            pl.BlockSpec((1, ROUTE_COLS), lambda i: (0, 0)),
        ],
        out_specs=[
            pl.BlockSpec((tm, d // 2), lambda i: (i, 0)),
            pl.BlockSpec((tm, ROUTE_COLS), lambda i: (i, 0)),
        ],
        out_shape=[
            jax.ShapeDtypeStruct((m, d // 2), jnp.uint32),
            jax.ShapeDtypeStruct((m, ROUTE_COLS), F32),
        ],
        compiler_params=_params(1, 40),
        name="norm_route",
    )(ya, yb, gain, mod, mod, w_route, b_route)


def _moe_kernel(te_ref, nu_ref, tok_ref, tok_next_ref, dst_ref, h_hbm, wg_ref, wu_ref, wd_ref, ys_hbm,
                gbuf, obuf, gsem, ssem, *, tm):
    del te_ref
    i = pl.program_id(0)
    n_used = nu_ref[0]
    n_grp = tm // SUBLANES
    spare_grp0 = ys_hbm.shape[0] - n_grp

    def row_of(hbm, row):
        return hbm.at[row >> (SUBLANES.bit_length() - 1), pl.ds(row & (SUBLANES - 1), 1)]

    def start_gather(ids_ref, slot):
        def body(g, carry):
            for s in range(SUBLANES):
                tok = ids_ref[0, 0, g * SUBLANES + s]
                pltpu.make_async_copy(row_of(h_hbm, tok), gbuf.at[slot, g, pl.ds(s, 1)], gsem.at[slot]).start()
            return carry

        lax.fori_loop(0, n_grp, body, 0)

    def wait_gather(slot):
        for g in range(n_grp):
            for s in range(SUBLANES):
                pltpu.make_async_copy(h_hbm.at[0, pl.ds(0, 1)], gbuf.at[slot, g, pl.ds(s, 1)], gsem.at[slot]).wait()

    def start_scatter():
        def body(g, carry):
            for s in range(SUBLANES):
                dst = dst_ref[0, 0, g * SUBLANES + s]
                pltpu.make_async_copy(obuf.at[g, pl.ds(s, 1)], row_of(ys_hbm, dst), ssem).start()
            return carry

        lax.fori_loop(0, n_grp, body, 0)

    def wait_scatter():
        for g in range(n_grp):
            for s in range(SUBLANES):
                pltpu.make_async_copy(obuf.at[g, pl.ds(s, 1)], ys_hbm.at[0, pl.ds(0, 1)], ssem).wait()

    @pl.when(i == 0)
    def _():
        obuf[...] = jnp.zeros_like(obuf)
        spare = pltpu.make_async_copy(obuf, ys_hbm.at[pl.ds(spare_grp0, n_grp)], ssem)
        spare.start()
        spare.wait()
        start_gather(tok_ref, 0)

    @pl.when(i < n_used)
    def _():
        slot = i % 2

        @pl.when(i + 1 < n_used)
        def _():
            start_gather(tok_next_ref, 1 - slot)

        wait_gather(slot)
        lo, hi = _unpack_bf16_pairs(gbuf[slot].reshape(tm, gbuf.shape[-1]))
        x = jnp.concatenate([lo, hi], axis=1).astype(BF16)
        a = _dot(x, wg_ref[0])
        u = _dot(x, wu_ref[0])
        hid = (a * _sigmoid(a)) * u
        out = _pack_bf16_pairs(_dot(hid.astype(BF16), wd_ref[0]))

        @pl.when(i > 0)
        def _():
            wait_scatter()

        obuf[...] = out.reshape(obuf.shape)
        start_scatter()

        @pl.when(i == n_used - 1)
        def _():
            wait_scatter()


def _moe_experts(h, tile_expert, n_used, row_tok, row_dst, w_gate, w_up, w_down, tm):
    n_tok, half = h.shape
    d = 2 * half
    n_tiles = row_tok.shape[0]
    f = w_gate.shape[-1]
    n_grp = tm // SUBLANES
    used = lambda i, nu: jnp.minimum(i, nu[0] - 1)
    ids = lambda tile_of: pl.BlockSpec((1, 1, tm), lambda i, te, nu: (tile_of(used(i, nu)), 0, 0),
                                       memory_space=pltpu.SMEM)
    grid_spec = pltpu.PrefetchScalarGridSpec(
        num_scalar_prefetch=2,
        grid=(n_tiles,),
        in_specs=[
            ids(lambda t: t),
            ids(lambda t: jnp.minimum(t + 1, n_tiles - 1)),
            ids(lambda t: t),
            pl.BlockSpec(memory_space=pl.ANY),
            pl.BlockSpec((1, d, f), lambda i, te, nu: (te[used(i, nu)], 0, 0)),
            pl.BlockSpec((1, d, f), lambda i, te, nu: (te[used(i, nu)], 0, 0)),
            pl.BlockSpec((1, f, d), lambda i, te, nu: (te[used(i, nu)], 0, 0)),
        ],
        out_specs=pl.BlockSpec(memory_space=pl.ANY),
        scratch_shapes=[
            pltpu.VMEM((2, n_grp, SUBLANES, half), jnp.uint32),
            pltpu.VMEM((n_grp, SUBLANES, half), jnp.uint32),
            pltpu.SemaphoreType.DMA((2,)),
            pltpu.SemaphoreType.DMA(()),
        ],
    )
    ys = pl.pallas_call(
        functools.partial(_moe_kernel, tm=tm),
        grid_spec=grid_spec,
        out_shape=jax.ShapeDtypeStruct(((2 * n_tok + tm) // SUBLANES, SUBLANES, half), jnp.uint32),
        compiler_params=_params(1, 56),
        name="moe_experts",
    )(tile_expert, n_used, row_tok, row_tok, row_dst, h.reshape(n_tok // SUBLANES, SUBLANES, half),
      w_gate, w_up, w_down)
    return ys.reshape(n_tok + tm // 2, d)


def _combine_kernel(y_ref, ys_ref, r_ref, g_ref, o_ref):
    half = y_ref.shape[-1] // 2
    r = r_ref[...]
    c1 = r[:, 2:3]
    c2 = r[:, 3:4]
    lo1, hi1 = _unpack_bf16_pairs(ys_ref[:, :half])
    lo2, hi2 = _unpack_bf16_pairs(ys_ref[:, half:])
    g = g_ref[0]
    o_ref[:, :half] = y_ref[:, :half] + g[:, :half] * (c1 * lo1 + c2 * lo2)
    o_ref[:, half:] = y_ref[:, half:] + g[:, half:] * (c1 * hi1 + c2 * hi2)


def _moe_combine(y2d, ys_tok, route, mod, row_of_tile, gate_idx, tm, tile0):
    m, d = y2d.shape
    return pl.pallas_call(
        _combine_kernel,
        grid=(m // tm,),
        in_specs=[
            pl.BlockSpec((tm, d), lambda i: (i, 0)),
            pl.BlockSpec((tm, d), lambda i: (tile0 + i, 0)),
            pl.BlockSpec((tm, ROUTE_COLS), lambda i: (tile0 + i, 0)),
            _mod_spec(d, row_of_tile, gate_idx),
        ],
        out_specs=pl.BlockSpec((tm, d), lambda i: (i, 0)),
        out_shape=jax.ShapeDtypeStruct((m, d), F32),
        compiler_params=_params(1, 40),
        name="moe_combine",
    )(y2d, ys_tok, route, mod)


def _routing_plan(route, tm):
    n_pair = 2 * route.shape[0]
    n_tiles = n_pair // tm + N_EXPERTS
    expert = route[:, 0:2].astype(jnp.int32).reshape(n_pair)
    order = jnp.argsort(expert, stable=True).astype(jnp.int32)
    counts = jnp.sum((expert[:, None] == jnp.arange(N_EXPERTS, dtype=jnp.int32)[None, :]).astype(jnp.int32),
                     axis=0)
    padded = ((counts + tm - 1) // tm) * tm
    pad_end = jnp.cumsum(padded)
    pad_start = pad_end - padded
    run_start = jnp.cumsum(counts) - counts
    tile_start = jnp.arange(n_tiles, dtype=jnp.int32) * tm
    tile_expert = jnp.minimum(jnp.sum((tile_start[:, None] >= pad_end[None, :]).astype(jnp.int32), axis=1),
                              N_EXPERTS - 1)
    rank = (tile_start - pad_start[tile_expert])[:, None] + jnp.arange(tm, dtype=jnp.int32)[None, :]
    valid = rank < counts[tile_expert][:, None]
    pair = order[jnp.clip(run_start[tile_expert][:, None] + rank, 0, n_pair - 1)]
    row_tok = jnp.where(valid, pair >> 1, 0).reshape(n_tiles, 1, tm)
    spare = n_pair + jnp.arange(tm, dtype=jnp.int32)[None, :]
    row_dst = jnp.where(valid, pair, spare).reshape(n_tiles, 1, tm)
    n_used = (pad_end[-1] // tm).astype(jnp.int32).reshape(1)
    return tile_expert.astype(jnp.int32), n_used, row_tok, row_dst


def _mixer(x, row_of_tokens, mod, attend, p):
    bsz, seq, d = x.shape
    m = bsz * seq
    x2d = x.reshape(m, d)
    conv_w = p["conv_dw"].shape[-1]
    att_w = (p["w_in"].shape[1] - 2 * conv_w) // 3
    n_heads = att_w // HEAD_DIM
    hpt = min(PROJ_HEADS, n_heads)
    tn = hpt * HEAD_DIM
    ptm = min(PROJ_TM, m)
    ntm = min(NORM_TM, m)

    def rows_of(tm):
        return lambda i: row_of_tokens(i * tm)

    want_cache = attend is None
    h1 = _norm_mod(x2d, p["norm1_g"], mod, rows_of(ntm), 1, 0, ntm)
    u = _glu_proj(h1, p["w_in"], conv_w, ptm, min(tn, conv_w))
    q, _ = _head_proj(h1, p["w_in"], 2 * conv_w, p["q_norm_g"], bsz, seq, n_heads, False, ptm, hpt)
    k, k_cache = _head_proj(h1, p["w_in"], 2 * conv_w + att_w, p["k_norm_g"], bsz, seq, n_heads, want_cache,
                            ptm, hpt)
    v, v_cache = _head_proj(h1, p["w_in"], 2 * conv_w + 2 * att_w, None, bsz, seq, n_heads, want_cache,
                            ptm, hpt)
    uc = _conv_module(u, seq, p["conv_dw"], p["conv_b"], p["conv_ln_g"], p["conv_ln_b"], min(CONV_TS, seq))
    if attend is None:
        att = _ctx_attention(q, k, v, hpt)
    else:
        att = attend(q, k, v)
    y = _out_proj(uc, att, p["w_out"], x2d, mod, rows_of(ptm), 2, ptm, min(tn, d))
    return y, k_cache, v_cache


def _routed_mlp(y_ctx, y_smp, mod, ctx_row, dec_s, p):
    m_ctx, d = y_ctx.shape
    m_smp = y_smp.shape[0]
    ntm = min(NORM_TM, m_ctx)
    n_a = m_ctx // ntm
    row_of_tile = lambda i: jnp.where(i < n_a, ctx_row, ((i - n_a) * ntm) // dec_s)
    h2, route = _norm_route(y_ctx, y_smp, p["norm2_g"], mod, row_of_tile, 4, 3, p["w_route"], p["b_route"], ntm)
    mtm = min(MOE_TM, m_ctx)
    tile_expert, n_used, row_tok, row_dst = _routing_plan(route, mtm)
    ys_tok = _moe_experts(h2, tile_expert, n_used, row_tok, row_dst, p["w_gate"], p["w_up"], p["w_down"], mtm)
    ctm = min(COMB_TM, m_ctx)
    out_ctx = _moe_combine(y_ctx, ys_tok, route, mod, lambda i: ctx_row, 5, ctm, 0)
    out_smp = _moe_combine(y_smp, ys_tok, route, mod, lambda i: (i * ctm) // dec_s, 5, ctm, m_ctx // ctm)
    return out_ctx, out_smp


def kernel(x_prompt, x_sample, cache_k, cache_v, c, c_ctx, norm1_g, w_mod, b_mod, w_in, conv_dw, conv_b,
           conv_ln_g, conv_ln_b, q_norm_g, k_norm_g, rpb, w_out, norm2_g, w_rg, b_rg, w_re, b_re, w_gate,
           w_up, w_down):
    depth = w_in.shape[0]
    assert depth == 1, "one trunk layer per step"
    d = x_prompt.shape[-1]
    dec_b, dec_s = x_sample.shape[0], x_sample.shape[1]
    assert dec_b < COND_ROWS
    ctx_row = dec_b

    cond = jnp.zeros((COND_ROWS, d), F32).at[:dec_b].set(c).at[ctx_row].set(c_ctx)
    mod = _modulation(cond, w_mod[0], b_mod[0][None, :]).reshape(COND_ROWS * N_MOD, 1, d)

    pad_cols = ROUTE_COLS - N_GROUPS - N_EXPERTS
    w_route = jnp.concatenate([w_rg[0], w_re[0], jnp.zeros((d, pad_cols), F32)], axis=1).astype(BF16)
    b_route = jnp.concatenate([b_rg[0], b_re[0], jnp.zeros((pad_cols,), F32)])[None, :]
    p = dict(
        norm1_g=norm1_g[0][None, :], norm2_g=norm2_g[0][None, :],
        w_in=w_in[0].astype(BF16), w_out=w_out[0].astype(BF16),
        conv_dw=conv_dw[0], conv_b=conv_b[0][None, :],
        conv_ln_g=conv_ln_g[0][None, :], conv_ln_b=conv_ln_b[0][None, :],
        q_norm_g=q_norm_g[0][None, :], k_norm_g=k_norm_g[0][None, :],
        w_route=w_route, b_route=b_route,
        w_gate=w_gate[0].astype(BF16), w_up=w_up[0].astype(BF16), w_down=w_down[0].astype(BF16),
    )

    y_ctx, k_new, v_new = _mixer(x_prompt, lambda tok: ctx_row, mod, None, p)
    attend = functools.partial(_na_attention, cache_k=cache_k, cache_v=cache_v, rpb=rpb[0])
    y_smp, _, _ = _mixer(x_sample, lambda tok: tok // dec_s, mod, attend, p)
    out_ctx, out_smp = _routed_mlp(y_ctx, y_smp, mod, ctx_row, dec_s, p)
    return (out_ctx.reshape(x_prompt.shape), out_smp.reshape(x_sample.shape), k_new, v_new)
```

```python
import functools

import numpy as np
import jax
import jax.numpy as jnp
from jax import lax
from jax.experimental import pallas as pl
from jax.experimental.pallas import tpu as pltpu

F32 = jnp.float32
BF16 = jnp.bfloat16

GRID_W = 64
WIN_R = 8
WIN_C = 16
CONV_K = 31
HEAD_DIM = 128
N_GROUPS = 4
EXPERTS_PER_GROUP = 8
N_EXPERTS = N_GROUPS * EXPERTS_PER_GROUP
N_MOD = 6
EPS = 1e-6
MASK_VALUE = -1e30
LOG2_E = 1.4426950408889634

LANES = 128
SUBLANES = 8
MXU_COLS = 256
MIB = 1024 * 1024

COND_ROWS = 16
MOD_TN = 512
NORM_TM = 256
PROJ_TM = 1024
PROJ_HEADS = 4
PROJ_HEADS_WIDE = 8
CONV_TS = 256
CONV_HALO = 16
CONV_RC = 128
NA_RB = 4
NA_UNROLL = 4
MOE_TM = 256
COMB_TM = 256
ROUTE_COLS = LANES


def _params(n_axes, vmem_mib):
    return pltpu.CompilerParams(
        dimension_semantics=("arbitrary",) * n_axes,
        vmem_limit_bytes=vmem_mib * MIB,
    )


def _sigmoid(x):
    return 1.0 / (1.0 + jnp.exp(-x))


def _dot(a, b):
    return jnp.dot(a, b, preferred_element_type=F32)


def _dot_t(a, b):
    return lax.dot_general(a, b, (((1,), (1,)), ((), ())), preferred_element_type=F32)


def _mod_kernel(c_ref, w_ref, b_ref, o_ref):
    c = c_ref[...]
    s = (c * _sigmoid(c)).astype(BF16)
    o_ref[...] = _dot(s, w_ref[...].astype(BF16)) + b_ref[...]


def _modulation(cond, w_mod, b_mod):
    d, n = w_mod.shape
    tn = min(MOD_TN, n)
    return pl.pallas_call(
        _mod_kernel,
        grid=(n // tn,),
        in_specs=[
            pl.BlockSpec((COND_ROWS, d), lambda j: (0, 0)),
            pl.BlockSpec((d, tn), lambda j: (0, j)),
            pl.BlockSpec((1, tn), lambda j: (0, j)),
        ],
        out_specs=pl.BlockSpec((COND_ROWS, tn), lambda j: (0, j)),
        out_shape=jax.ShapeDtypeStruct((COND_ROWS, n), F32),
        compiler_params=_params(1, 40),
        name="modulation",
    )(cond, w_mod, b_mod)


def _rms_mod(x, g, sc, sh):
    y = x * lax.rsqrt(jnp.mean(x * x, axis=-1, keepdims=True) + EPS) * g
    return y * (1.0 + sc) + sh


def _norm_mod_kernel(x_ref, g_ref, sc_ref, sh_ref, o_ref):
    o_ref[...] = _rms_mod(x_ref[...], g_ref[...], sc_ref[0], sh_ref[0]).astype(o_ref.dtype)


def _mod_spec(d, row_of_tile, which):
    return pl.BlockSpec((1, 1, d), lambda i, *_: (row_of_tile(i) * N_MOD + which, 0, 0))


def _norm_mod(x2d, gain, mod, row_of_tile, sc_idx, sh_idx, tm):
    m, d = x2d.shape
    return pl.pallas_call(
        _norm_mod_kernel,
        grid=(m // tm,),
        in_specs=[
            pl.BlockSpec((tm, d), lambda i: (i, 0)),
            pl.BlockSpec((1, d), lambda i: (0, 0)),
            _mod_spec(d, row_of_tile, sc_idx),
            _mod_spec(d, row_of_tile, sh_idx),
        ],
        out_specs=pl.BlockSpec((tm, d), lambda i: (i, 0)),
        out_shape=jax.ShapeDtypeStruct((m, d), BF16),
        compiler_params=_params(1, 32),
        name="norm_mod",
    )(x2d, gain, mod, mod)


def _glu_kernel(h_ref, wa_ref, wg_ref, o_ref):
    h = h_ref[...]
    a = _dot(h, wa_ref[...])
    g = _dot(h, wg_ref[...])
    o_ref[...] = a * _sigmoid(g)


def _glu_proj(h, w_in, conv_w, tm, tn):
    m, d = h.shape
    nj = conv_w // tn
    return pl.pallas_call(
        _glu_kernel,
        grid=(m // tm, nj),
        in_specs=[
            pl.BlockSpec((tm, d), lambda i, j: (i, 0)),
            pl.BlockSpec((d, tn), lambda i, j: (0, j)),
            pl.BlockSpec((d, tn), lambda i, j: (0, nj + j)),
        ],
        out_specs=pl.BlockSpec((tm, tn), lambda i, j: (i, j)),
        out_shape=jax.ShapeDtypeStruct((m, conv_w), F32),
        compiler_params=_params(2, 48),
        name="glu_proj",
    )(h, w_in, w_in)


def _head_proj_kernel(h_ref, w_ref, *refs, hpt, nb, sb, normalize):
    if normalize:
        g_ref, o_ref, *cache_refs = refs
    else:
        o_ref, *cache_refs = refs
    acc = _dot(h_ref[...], w_ref[...])
    for hh in range(hpt):
        z = acc[:, hh * HEAD_DIM:(hh + 1) * HEAD_DIM]
        if normalize:
            z = z * lax.rsqrt(jnp.mean(z * z, axis=-1, keepdims=True) + EPS) * g_ref[...]
        for b in range(nb):
            blk = z[b * sb:(b + 1) * sb]
            o_ref[b, hh] = blk.astype(o_ref.dtype)
            if cache_refs:
                cache_refs[0][b, 0, hh] = blk


def _head_proj(h, w_in, col0, gain, bsz, seq, n_heads, want_cache, tm):
    m, d = h.shape
    hpt = min(PROJ_HEADS if want_cache else PROJ_HEADS_WIDE, n_heads)
    tn = hpt * HEAD_DIM
    jb0 = col0 // tn
    if seq >= tm:
        nb, sb, spb = 1, tm, seq // tm
        o_map = lambda i, j: (i // spb, j, i % spb, 0)
        c_map = lambda i, j: (i // spb, 0, j, i % spb, 0)
    else:
        nb, sb = tm // seq, seq
        o_map = lambda i, j: (i, j, 0, 0)
        c_map = lambda i, j: (i, 0, j, 0, 0)
    out_shape = [jax.ShapeDtypeStruct((bsz, n_heads, seq, HEAD_DIM), BF16)]
    out_specs = [pl.BlockSpec((nb, hpt, sb, HEAD_DIM), o_map)]
    if want_cache:
        out_shape.append(jax.ShapeDtypeStruct((bsz, 1, n_heads, seq, HEAD_DIM), F32))
        out_specs.append(pl.BlockSpec((nb, 1, hpt, sb, HEAD_DIM), c_map))
    normalize = gain is not None
    in_specs = [
        pl.BlockSpec((tm, d), lambda i, j: (i, 0)),
        pl.BlockSpec((d, tn), lambda i, j: (0, jb0 + j)),
    ]
    operands = [h, w_in]
    if normalize:
        in_specs.append(pl.BlockSpec((1, HEAD_DIM), lambda i, j: (0, 0)))
        operands.append(gain)
    res = pl.pallas_call(
        functools.partial(_head_proj_kernel, hpt=hpt, nb=nb, sb=sb, normalize=normalize),
        grid=(m // tm, n_heads // hpt),
        in_specs=in_specs,
        out_specs=out_specs,
        out_shape=out_shape,
        compiler_params=_params(2, 48),
        name="head_proj",
    )(*operands)
    return res if want_cache else (res[0], None)


def _conv_kernel(u_ref, up_ref, un_ref, w_ref, b_ref, g_ref, beta_ref, o_ref, pad_ref, y_ref, *, ts, n_s):
    s = pl.program_id(1)
    c = u_ref.shape[-1]
    pad_ref[0:CONV_HALO, :] = jnp.where(s == 0, 0.0, up_ref[...])
    pad_ref[CONV_HALO:CONV_HALO + ts, :] = u_ref[...]
    pad_ref[CONV_HALO + ts:2 * CONV_HALO + ts, :] = jnp.where(s == n_s - 1, 0.0, un_ref[...])
    rc = min(CONV_RC, ts)
    shift = CONV_HALO - CONV_K // 2
    for c0 in range(0, c, LANES):
        wv = w_ref[:, c0:c0 + LANES]
        bv = b_ref[:, c0:c0 + LANES]

        def body(ci, carry, c0=c0, wv=wv, bv=bv):
            r0 = pl.multiple_of(ci * rc, rc)
            chunk = pad_ref[pl.ds(r0, rc + 2 * CONV_HALO), c0:c0 + LANES]
            acc = jnp.zeros((rc, LANES), F32)
            for j in range(SUBLANES):
                tiles = [(k, (k + shift) // SUBLANES) for k in range(CONV_K) if (k + shift) % SUBLANES == j]
                if not tiles:
                    continue
                shifted = chunk if j == 0 else pltpu.roll(chunk, chunk.shape[0] - j, axis=0)
                for k, a in tiles:
                    acc = acc + wv[k:k + 1, :] * shifted[a * SUBLANES:a * SUBLANES + rc, :]
            y_ref[pl.ds(r0, rc), c0:c0 + LANES] = acc + bv
            return carry

        lax.fori_loop(0, ts // rc, body, 0)
    y = y_ref[...]
    mu = jnp.mean(y, axis=-1, keepdims=True)
    yc = y - mu
    var = jnp.mean(yc * yc, axis=-1, keepdims=True)
    z = yc * lax.rsqrt(var + EPS) * g_ref[...] + beta_ref[...]
    o_ref[...] = (z * _sigmoid(z)).astype(o_ref.dtype)


def _conv_module(u, seq, conv_dw, conv_b, ln_g, ln_b, ts):
    m, c = u.shape
    n_s = seq // ts
    hb = ts // CONV_HALO
    n_hblk = m // CONV_HALO
    main = lambda b, s: (b * n_s + s, 0)
    prev = lambda b, s: (jnp.maximum((b * n_s + s) * hb - 1, 0), 0)
    nxt = lambda b, s: (jnp.minimum((b * n_s + s + 1) * hb, n_hblk - 1), 0)
    vec = lambda rows: pl.BlockSpec((rows, c), lambda b, s: (0, 0))
    return pl.pallas_call(
        functools.partial(_conv_kernel, ts=ts, n_s=n_s),
        grid=(m // seq, n_s),
        in_specs=[
            pl.BlockSpec((ts, c), main),
            pl.BlockSpec((CONV_HALO, c), prev),
            pl.BlockSpec((CONV_HALO, c), nxt),
            vec(CONV_K), vec(1), vec(1), vec(1),
        ],
        out_specs=pl.BlockSpec((ts, c), main),
        out_shape=jax.ShapeDtypeStruct((m, c), BF16),
        scratch_shapes=[
            pltpu.VMEM((ts + 2 * CONV_HALO, c), F32),
            pltpu.VMEM((ts, c), F32),
        ],
        compiler_params=_params(2, 40),
        name="conv_module",
    )(u, u, u, conv_dw, conv_b, ln_g, ln_b)


def _ctx_attn_kernel(q_ref, k_ref, v_ref, o_ref, *, hpt):
    for hh in range(hpt):
        q = q_ref[0, hh]
        k = k_ref[0, hh]
        v = v_ref[0, hh]
        s = _dot_t(q, k)
        p = jnp.exp2(s - jnp.max(s, axis=-1, keepdims=True))
        l = jnp.sum(p, axis=-1, keepdims=True)
        o = _dot(p.astype(BF16), v) / l
        o_ref[:, hh * HEAD_DIM:(hh + 1) * HEAD_DIM] = o.astype(o_ref.dtype)


def _ctx_attention(q, k, v, hpt):
    bsz, nh, seq, _ = q.shape
    spec = pl.BlockSpec((1, hpt, seq, HEAD_DIM), lambda b, j: (b, j, 0, 0))
    return pl.pallas_call(
        functools.partial(_ctx_attn_kernel, hpt=hpt),
        grid=(bsz, nh // hpt),
        in_specs=[spec, spec, spec],
        out_specs=pl.BlockSpec((seq, hpt * HEAD_DIM), lambda b, j: (b, j)),
        out_shape=jax.ShapeDtypeStruct((bsz * seq, nh * HEAD_DIM), BF16),
        compiler_params=_params(2, 32),
        name="ctx_attention",
    )(q, k, v)


def _na_block_plan(rows, kr):
    slab = kr + NA_RB - 1
    n_blk = rows // NA_RB
    assert rows % NA_RB == 0 and rows >= slab and n_blk >= 3
    plans = []
    for b in range(n_blk):
        ks = min(max(b * NA_RB - kr // 2, 0), rows - slab)
        plan = np.full((NA_RB, slab), -1, np.int64)
        for rho in range(NA_RB):
            r = b * NA_RB + rho
            start = min(max(r - kr // 2, 0), rows - kr)
            for i in range(slab):
                j = ks + i
                if start <= j < start + kr:
                    plan[rho, i] = j - r + WIN_R - 1
        plans.append(plan)
    for b in range(2, n_blk - 1):
        assert (plans[b] == plans[1]).all()
    return np.stack([plans[0], plans[1], plans[-1]])


def _na_kernel(q_ref, k_ref, v_ref, ck_ref, cv_ref, t_ref, o_ref, *, rows, kr):
    ck = ck_ref[0, 0, 0].astype(BF16)
    cv = cv_ref[0, 0, 0].astype(BF16)
    slab = kr + NA_RB - 1
    n_blk = rows // NA_RB
    nq = NA_RB * GRID_W
    nk = slab * GRID_W

    def body(b, carry):
        ks = jnp.clip(b * NA_RB - kr // 2, 0, rows - slab)
        kind = jnp.where(b == 0, 0, jnp.where(b == n_blk - 1, 2, 1))
        q0 = pl.multiple_of(b * nq, nq)
        k0 = pl.multiple_of(ks * GRID_W, GRID_W)
        qb = q_ref[0, 0, pl.ds(q0, nq), :]
        kb = k_ref[0, 0, pl.ds(k0, nk), :]
        vb = v_ref[0, 0, pl.ds(k0, nk), :]
        s_lat = _dot_t(qb, kb) + t_ref[0, kind]
        s_ctx = _dot_t(qb, ck)
        mx = jnp.maximum(jnp.max(s_lat, axis=-1, keepdims=True), jnp.max(s_ctx, axis=-1, keepdims=True))
        p_lat = jnp.exp2(s_lat - mx)
        p_ctx = jnp.exp2(s_ctx - mx)
        l = jnp.sum(p_lat, axis=-1, keepdims=True) + jnp.sum(p_ctx, axis=-1, keepdims=True)
        o = (_dot(p_lat.astype(BF16), vb) + _dot(p_ctx.astype(BF16), cv)) / l
        o_ref[pl.ds(q0, nq), :] = o.astype(o_ref.dtype)
        return carry

    lax.fori_loop(0, n_blk, body, 0, unroll=NA_UNROLL)


def _na_bias_table(rpb, rows, kr):
    plan = _na_block_plan(rows, kr)
    slab = plan.shape[-1]
    cols = np.arange(GRID_W)
    c_start = np.clip(cols - WIN_C // 2, 0, GRID_W - WIN_C)
    col_mask = (cols[None, :] >= c_start[:, None]) & (cols[None, :] < c_start[:, None] + WIN_C)
    dc_idx = np.clip(cols[None, :] - cols[:, None], -(WIN_C - 1), WIN_C - 1) + WIN_C - 1
    row_sel = (plan[..., None] == np.arange(2 * WIN_R - 1)).astype(np.float32)
    col_sel = (dc_idx[..., None] == np.arange(2 * WIN_C - 1)).astype(np.float32)
    bias = jnp.einsum("vpid,hdc,qkc->hvpqik", row_sel, rpb, col_sel, precision=lax.Precision.HIGHEST)
    valid = (plan >= 0)[None, :, :, None, :, None] & col_mask[None, None, None, :, None, :]
    bias = jnp.where(valid, bias * LOG2_E, MASK_VALUE)
    return bias.reshape(rpb.shape[0], 3, NA_RB * GRID_W, slab * GRID_W)


def _na_attention(q, k, v, cache_k, cache_v, rpb):
    bsz, nh, seq, _ = q.shape
    rows = seq // GRID_W
    kr = min(WIN_R, rows)
    past = cache_k.shape[3]
    table = _na_bias_table(rpb, rows, kr)
    qkv = pl.BlockSpec((1, 1, seq, HEAD_DIM), lambda h, b: (b, h, 0, 0))
    cache = pl.BlockSpec((1, 1, 1, past, HEAD_DIM), lambda h, b: (b, 0, h, 0, 0))
    return pl.pallas_call(
        functools.partial(_na_kernel, rows=rows, kr=kr),
        grid=(nh, bsz),
        in_specs=[qkv, qkv, qkv, cache, cache,
                  pl.BlockSpec((1,) + table.shape[1:], lambda h, b: (h, 0, 0, 0))],
        out_specs=pl.BlockSpec((seq, HEAD_DIM), lambda h, b: (b, h)),
        out_shape=jax.ShapeDtypeStruct((bsz * seq, nh * HEAD_DIM), BF16),
        compiler_params=_params(2, 32),
        name="na_attention",
    )(q, k, v, cache_k, cache_v, table)


def _out_proj_kernel(uc_ref, att_ref, wt_ref, wb_ref, x_ref, g_ref, o_ref):
    acc = _dot(uc_ref[...], wt_ref[...]) + _dot(att_ref[...], wb_ref[...])
    o_ref[...] = x_ref[...] + g_ref[0] * acc


def _out_proj(uc, att, w_out, x2d, mod, row_of_tile, gate_idx, tm, tn):
    m, d = x2d.shape
    cw = uc.shape[1]
    aw = att.shape[1]
    assert cw == aw
    return pl.pallas_call(
        _out_proj_kernel,
        grid=(m // tm, d // tn),
        in_specs=[
            pl.BlockSpec((tm, cw), lambda i, j: (i, 0)),
            pl.BlockSpec((tm, aw), lambda i, j: (i, 0)),
            pl.BlockSpec((cw, tn), lambda i, j: (0, j)),
            pl.BlockSpec((aw, tn), lambda i, j: (1, j)),
            pl.BlockSpec((tm, tn), lambda i, j: (i, j)),
            pl.BlockSpec((1, 1, tn), lambda i, j: (row_of_tile(i) * N_MOD + gate_idx, 0, j)),
        ],
        out_specs=pl.BlockSpec((tm, tn), lambda i, j: (i, j)),
        out_shape=jax.ShapeDtypeStruct((m, d), F32),
        compiler_params=_params(2, 48),
        name="out_proj",
    )(uc, att, w_out, w_out, x2d, mod)


def _pack_bf16_pairs(x):
    half = x.shape[-1] // 2
    lo = lax.bitcast_convert_type(x[:, :half].astype(BF16).astype(F32), jnp.uint32)
    hi = lax.bitcast_convert_type(x[:, half:].astype(BF16).astype(F32), jnp.uint32)
    return (lo >> 16) | (hi & jnp.uint32(0xFFFF0000))


def _unpack_bf16_pairs(w):
    lo = lax.bitcast_convert_type(w << 16, F32)
    hi = lax.bitcast_convert_type(w & jnp.uint32(0xFFFF0000), F32)
    return lo, hi


def _store_token_slabs(ref, first_chunk, packed):
    rows = packed.shape[0]
    n_chunk = ref.shape[0] // rows
    for k in range(packed.shape[1] // LANES):
        ref[pl.ds(first_chunk + k, rows, stride=n_chunk), :] = packed[:, k * LANES:(k + 1) * LANES]


def _load_token_slabs(ref, rows):
    n_chunk = ref.shape[0] // rows
    return [ref[pl.ds(c, rows, stride=n_chunk), :] for c in range(n_chunk)]


def _norm_route_kernel(ya_ref, yb_ref, g_ref, sc_ref, sh_ref, wr_ref, br_ref, h_ref, r_ref, *, n_a):
    i = pl.program_id(0)

    @pl.when(i < n_a)
    def _():
        _norm_route_tile(ya_ref, g_ref, sc_ref, sh_ref, wr_ref, br_ref, h_ref, r_ref)

    @pl.when(i >= n_a)
    def _():
        _norm_route_tile(yb_ref, g_ref, sc_ref, sh_ref, wr_ref, br_ref, h_ref, r_ref)


def _norm_route_tile(y_ref, g_ref, sc_ref, sh_ref, wr_ref, br_ref, h_ref, r_ref):
    h = _rms_mod(y_ref[...], g_ref[...], sc_ref[0], sh_ref[0])
    _store_token_slabs(h_ref, 0, _pack_bf16_pairs(h))
    logits = _dot(h.astype(BF16), wr_ref[...]) + br_ref[...]
    lane = lax.broadcasted_iota(jnp.int32, logits.shape, 1)
    lane_f = lane.astype(F32)
    no_lane = float(ROUTE_COLS)

    def first_argmax(mask):
        vals = jnp.where(mask, logits, MASK_VALUE)
        top = jnp.max(vals, axis=-1, keepdims=True)
        idx = jnp.min(jnp.where(mask & (logits == top), lane_f, no_lane), axis=-1, keepdims=True)
        return vals, top, idx

    g_mask = lane < N_GROUPS
    g_vals, g_top, g_idx = first_argmax(g_mask)
    g_prob = 1.0 / jnp.sum(jnp.where(g_mask, jnp.exp(g_vals - g_top), 0.0), axis=-1, keepdims=True)
    lo = N_GROUPS + EXPERTS_PER_GROUP * g_idx
    e_mask = (lane_f >= lo) & (lane_f < lo + EXPERTS_PER_GROUP)
    _, top1, idx1 = first_argmax(e_mask)
    _, top2, idx2 = first_argmax(e_mask & (lane_f != idx1))
    t = jnp.exp(top2 - top1)
    c1 = g_prob / (1.0 + t)
    c2 = g_prob * t / (1.0 + t)
    e1 = idx1 - N_GROUPS
    e2 = idx2 - N_GROUPS
    r_ref[...] = jnp.where(lane == 0, e1, jnp.where(lane == 1, e2, jnp.where(lane == 2, c1,
                           jnp.where(lane == 3, c2, 0.0))))


def _norm_route(ya, yb, gain, mod, row_of_tile, sc_idx, sh_idx, w_route, b_route, tm):
    d = ya.shape[1]
    n_chunk = d // 2 // LANES
    n_a = ya.shape[0] // tm
    m = ya.shape[0] + yb.shape[0]
    return pl.pallas_call(
        functools.partial(_norm_route_kernel, n_a=n_a),
        grid=(m // tm,),
        in_specs=[
            pl.BlockSpec((tm, d), lambda i: (jnp.minimum(i, n_a - 1), 0)),
            pl.BlockSpec((tm, d), lambda i: (jnp.maximum(i - n_a, 0), 0)),
            pl.BlockSpec((1, d), lambda i: (0, 0)),
            _mod_spec(d, row_of_tile, sc_idx),
            _mod_spec(d, row_of_tile, sh_idx),
            pl.BlockSpec((d, ROUTE_COLS), lambda i: (0, 0)),
            pl.BlockSpec((1, ROUTE_COLS), lambda i: (0, 0)),
        ],
        out_specs=[
            pl.BlockSpec((tm * n_chunk, LANES), lambda i: (i, 0)),
            pl.BlockSpec((tm, ROUTE_COLS), lambda i: (i, 0)),
        ],
        out_shape=[
            jax.ShapeDtypeStruct((m * n_chunk, LANES), jnp.uint32),
            jax.ShapeDtypeStruct((m, ROUTE_COLS), F32),
        ],
        compiler_params=_params(1, 40),
        name="norm_route",
    )(ya, yb, gain, mod, mod, w_route, b_route)


def _moe_kernel(te_ref, nu_ref, tok_ref, tok_next_ref, dst_prev_ref, dst_ref, h_hbm, wg_ref, wu_ref, wd_ref,
                ys_hbm, gbuf, obuf, gsem, ssem, *, tm):
    del te_ref
    i = pl.program_id(0)
    n_used = nu_ref[0]
    n_chunk = gbuf.shape[1] // tm
    half = n_chunk * LANES
    spare_row0 = ys_hbm.shape[0] // n_chunk - tm

    def slab(ref, row):
        if not isinstance(row, int):
            row = pl.multiple_of(row * n_chunk, n_chunk)
        else:
            row = row * n_chunk
        return ref.at[pl.ds(row, n_chunk)]

    def gather_copy(tok, slot, r):
        return pltpu.make_async_copy(slab(h_hbm, tok), slab(gbuf.at[slot], r), gsem.at[slot])

    def scatter_copy(slot, r, dst):
        return pltpu.make_async_copy(slab(obuf.at[slot], r), slab(ys_hbm, dst), ssem.at[slot])

    def wait_gather(slot):
        for r in range(tm):
            gather_copy(0, slot, r).wait()

    def wait_scatter(slot):
        for r in range(tm):
            scatter_copy(slot, r, 0).wait()

    @pl.when(i == 0)
    def _():
        obuf[...] = jnp.zeros_like(obuf)
        spare = pltpu.make_async_copy(obuf.at[0], ys_hbm.at[pl.ds(spare_row0 * n_chunk, tm * n_chunk)], ssem.at[0])
        spare.start()
        spare.wait()

        def first_gather(r, carry):
            gather_copy(tok_ref[0, 0, r], 0, r).start()
            return carry

        lax.fori_loop(0, tm, first_gather, 0)

    @pl.when(i < n_used)
    def _():
        slot = i % 2
        other = 1 - slot
        f = wg_ref.shape[-1]
        first = i == 0

        def issue(r):
            gather_copy(tok_next_ref[0, 0, r], other, r).start(priority=r % 2)
            dst = jnp.where(first, spare_row0 + r, dst_prev_ref[0, 0, r])
            scatter_copy(other, r, dst).start(priority=(r + 1) % 2)

        gu_cols = min(MXU_COLS, f)
        out_cols = min(MXU_COLS, half)
        n_gu = f // gu_cols
        n_out = half // out_cols
        weights = [4] * n_gu + [1] * n_out
        bounds = np.round(np.cumsum([0] + weights) * (tm / sum(weights))).astype(int)
        stage = iter(range(len(weights)))

        def issue_stage():
            k = next(stage)
            for r in range(bounds[k], bounds[k + 1]):
                issue(r)

        wait_gather(slot)

        @pl.when(i > 0)
        def _():
            wait_scatter(slot)

        words = [_unpack_bf16_pairs(w) for w in _load_token_slabs(gbuf.at[slot], tm)]
        x = jnp.concatenate([lo for lo, _ in words] + [hi for _, hi in words], axis=1).astype(BF16)
        hid = []
        for c in range(n_gu):
            cols = slice(c * gu_cols, (c + 1) * gu_cols)
            a = _dot(x, wg_ref[0, :, cols])
            u = _dot(x, wu_ref[0, :, cols])
            hid.append(((a * _sigmoid(a)) * u).astype(BF16))
            issue_stage()
        hid = jnp.concatenate(hid, axis=1)
        for c in range(n_out):
            cols_lo = slice(c * out_cols, (c + 1) * out_cols)
            cols_hi = slice(half + c * out_cols, half + (c + 1) * out_cols)
            o_lo = _dot(hid, wd_ref[0, :, cols_lo])
            o_hi = _dot(hid, wd_ref[0, :, cols_hi])
            packed = _pack_bf16_pairs(jnp.concatenate([o_lo, o_hi], axis=1))
            _store_token_slabs(obuf.at[slot], c * (out_cols // LANES), packed)
            issue_stage()

        @pl.when(i == n_used - 1)
        def _():
            wait_gather(other)
            wait_scatter(other)

            def last_scatter(r, carry):
                scatter_copy(slot, r, dst_ref[0, 0, r]).start()
                return carry

            lax.fori_loop(0, tm, last_scatter, 0)
            wait_scatter(slot)


def _moe_experts(h, tile_expert, n_used, row_tok, row_dst, w_gate, w_up, w_down, tm):
    _, d, f = w_gate.shape
    n_chunk = d // 2 // LANES
    n_tok = h.shape[0] // n_chunk
    n_tiles = row_tok.shape[0]
    used = lambda i, nu: jnp.minimum(i, nu[0] - 1)
    ids = lambda tile_of: pl.BlockSpec((1, 1, tm), lambda i, te, nu: (tile_of(used(i, nu)), 0, 0),
                                       memory_space=pltpu.SMEM)
    grid_spec = pltpu.PrefetchScalarGridSpec(
        num_scalar_prefetch=2,
        grid=(n_tiles,),
        in_specs=[
            ids(lambda t: t),
            ids(lambda t: jnp.minimum(t + 1, n_tiles - 1)),
            ids(lambda t: jnp.maximum(t - 1, 0)),
            ids(lambda t: t),
            pl.BlockSpec(memory_space=pl.ANY),
            pl.BlockSpec((1, d, f), lambda i, te, nu: (te[used(i, nu)], 0, 0)),
            pl.BlockSpec((1, d, f), lambda i, te, nu: (te[used(i, nu)], 0, 0)),
            pl.BlockSpec((1, f, d), lambda i, te, nu: (te[used(i, nu)], 0, 0)),
        ],
        out_specs=pl.BlockSpec(memory_space=pl.ANY),
        scratch_shapes=[
            pltpu.VMEM((2, tm * n_chunk, LANES), jnp.uint32),
            pltpu.VMEM((2, tm * n_chunk, LANES), jnp.uint32),
            pltpu.SemaphoreType.DMA((2,)),
            pltpu.SemaphoreType.DMA((2,)),
        ],
    )
    return pl.pallas_call(
        functools.partial(_moe_kernel, tm=tm),
        grid_spec=grid_spec,
        out_shape=jax.ShapeDtypeStruct(((2 * n_tok + tm) * n_chunk, LANES), jnp.uint32),
        compiler_params=_params(1, 56),
        name="moe_experts",
    )(tile_expert, n_used, row_tok, row_tok, row_dst, row_dst, h, w_gate, w_up, w_down)


def _combine_kernel(y_ref, ys1_ref, ys2_ref, r_ref, g_ref, o_ref):
    tm, d = y_ref.shape
    half = d // 2
    r = r_ref[...]
    c1 = r[:, 2:3]
    c2 = r[:, 3:4]
    g = g_ref[0]
    slabs = zip(_load_token_slabs(ys1_ref, tm), _load_token_slabs(ys2_ref, tm))
    for c, (w1, w2) in enumerate(slabs):
        lo1, hi1 = _unpack_bf16_pairs(w1)
        lo2, hi2 = _unpack_bf16_pairs(w2)
        lo = slice(c * LANES, (c + 1) * LANES)
        hi = slice(half + c * LANES, half + (c + 1) * LANES)
        o_ref[:, lo] = y_ref[:, lo] + g[:, lo] * (c1 * lo1 + c2 * lo2)
        o_ref[:, hi] = y_ref[:, hi] + g[:, hi] * (c1 * hi1 + c2 * hi2)


def _moe_combine(y2d, ys, route, mod, row_of_tile, gate_idx, tm, tile0):
    m, d = y2d.shape
    slot1 = route.shape[0] // tm
    n_chunk = d // 2 // LANES
    return pl.pallas_call(
        _combine_kernel,
        grid=(m // tm,),
        in_specs=[
            pl.BlockSpec((tm, d), lambda i: (i, 0)),
            pl.BlockSpec((tm * n_chunk, LANES), lambda i: (tile0 + i, 0)),
            pl.BlockSpec((tm * n_chunk, LANES), lambda i: (slot1 + tile0 + i, 0)),
            pl.BlockSpec((tm, ROUTE_COLS), lambda i: (tile0 + i, 0)),
            _mod_spec(d, row_of_tile, gate_idx),
        ],
        out_specs=pl.BlockSpec((tm, d), lambda i: (i, 0)),
        out_shape=jax.ShapeDtypeStruct((m, d), F32),
        compiler_params=_params(1, 40),
        name="moe_combine",
    )(y2d, ys, ys, route, mod)


def _routing_plan(route, tm):
    n_pair = 2 * route.shape[0]
    n_tiles = n_pair // tm + N_EXPERTS
    expert = route[:, 0:2].astype(jnp.int32).reshape(n_pair)
    order = jnp.argsort(expert, stable=True).astype(jnp.int32)
    counts = jnp.sum((expert[:, None] == jnp.arange(N_EXPERTS, dtype=jnp.int32)[None, :]).astype(jnp.int32),
                     axis=0)
    padded = ((counts + tm - 1) // tm) * tm
    pad_end = jnp.cumsum(padded)
    pad_start = pad_end - padded
    run_start = jnp.cumsum(counts) - counts
    tile_start = jnp.arange(n_tiles, dtype=jnp.int32) * tm
    tile_expert = jnp.minimum(jnp.sum((tile_start[:, None] >= pad_end[None, :]).astype(jnp.int32), axis=1),
                              N_EXPERTS - 1)
    rank = (tile_start - pad_start[tile_expert])[:, None] + jnp.arange(tm, dtype=jnp.int32)[None, :]
    valid = rank < counts[tile_expert][:, None]
    pair = order[jnp.clip(run_start[tile_expert][:, None] + rank, 0, n_pair - 1)]
    row_tok = jnp.where(valid, pair >> 1, 0).reshape(n_tiles, 1, tm)
    spare = n_pair + jnp.arange(tm, dtype=jnp.int32)[None, :]
    row_dst = jnp.where(valid, (pair & 1) * (n_pair // 2) + (pair >> 1), spare).reshape(n_tiles, 1, tm)
    n_used = (pad_end[-1] // tm).astype(jnp.int32).reshape(1)
    return tile_expert.astype(jnp.int32), n_used, row_tok, row_dst


def _mixer(x, row_of_tokens, mod, attend, p):
    bsz, seq, d = x.shape
    m = bsz * seq
    x2d = x.reshape(m, d)
    conv_w = p["conv_dw"].shape[-1]
    att_w = (p["w_in"].shape[1] - 2 * conv_w) // 3
    n_heads = att_w // HEAD_DIM
    hpt = min(PROJ_HEADS, n_heads)
    tn = hpt * HEAD_DIM
    ptm = min(PROJ_TM, m)
    ntm = min(NORM_TM, m)

    def rows_of(tm):
        return lambda i: row_of_tokens(i * tm)

    want_cache = attend is None
    h1 = _norm_mod(x2d, p["norm1_g"], mod, rows_of(ntm), 1, 0, ntm)
    u = _glu_proj(h1, p["w_in"], conv_w, ptm, min(tn, conv_w))
    q, _ = _head_proj(h1, p["w_in"], 2 * conv_w, p["q_norm_g"], bsz, seq, n_heads, False, ptm)
    k, k_cache = _head_proj(h1, p["w_in"], 2 * conv_w + att_w, p["k_norm_g"], bsz, seq, n_heads, want_cache,
                            ptm)
    v, v_cache = _head_proj(h1, p["w_in"], 2 * conv_w + 2 * att_w, None, bsz, seq, n_heads, want_cache,
                            ptm)
    uc = _conv_module(u, seq, p["conv_dw"], p["conv_b"], p["conv_ln_g"], p["conv_ln_b"], min(CONV_TS, seq))
    if attend is None:
        att = _ctx_attention(q, k, v, hpt)
    else:
        att = attend(q, k, v)
    y = _out_proj(uc, att, p["w_out"], x2d, mod, rows_of(ptm), 2, ptm, min(tn, d))
    return y, k_cache, v_cache


def _routed_mlp(y_ctx, y_smp, mod, ctx_row, dec_s, p):
    m_ctx, d = y_ctx.shape
    m_smp = y_smp.shape[0]
    ntm = min(NORM_TM, m_ctx)
    n_a = m_ctx // ntm
    row_of_tile = lambda i: jnp.where(i < n_a, ctx_row, ((i - n_a) * ntm) // dec_s)
    h2, route = _norm_route(y_ctx, y_smp, p["norm2_g"], mod, row_of_tile, 4, 3, p["w_route"], p["b_route"], ntm)
    mtm = min(MOE_TM, m_ctx)
    tile_expert, n_used, row_tok, row_dst = _routing_plan(route, mtm)
    ys_tok = _moe_experts(h2, tile_expert, n_used, row_tok, row_dst, p["w_gate"], p["w_up"], p["w_down"], mtm)
    ctm = min(COMB_TM, m_ctx)
    out_ctx = _moe_combine(y_ctx, ys_tok, route, mod, lambda i: ctx_row, 5, ctm, 0)
    out_smp = _moe_combine(y_smp, ys_tok, route, mod, lambda i: (i * ctm) // dec_s, 5, ctm, m_ctx // ctm)
    return out_ctx, out_smp


def kernel(x_prompt, x_sample, cache_k, cache_v, c, c_ctx, norm1_g, w_mod, b_mod, w_in, conv_dw, conv_b,
           conv_ln_g, conv_ln_b, q_norm_g, k_norm_g, rpb, w_out, norm2_g, w_rg, b_rg, w_re, b_re, w_gate,
           w_up, w_down):
    depth = w_in.shape[0]
    assert depth == 1, "one trunk layer per step"
    d = x_prompt.shape[-1]
    dec_b, dec_s = x_sample.shape[0], x_sample.shape[1]
    assert dec_b < COND_ROWS
    ctx_row = dec_b

    cond = jnp.zeros((COND_ROWS, d), F32).at[:dec_b].set(c).at[ctx_row].set(c_ctx)
    mod = _modulation(cond, w_mod[0], b_mod[0][None, :]).reshape(COND_ROWS * N_MOD, 1, d)

    pad_cols = ROUTE_COLS - N_GROUPS - N_EXPERTS
    w_route = jnp.concatenate([w_rg[0], w_re[0], jnp.zeros((d, pad_cols), F32)], axis=1).astype(BF16)
    b_route = jnp.concatenate([b_rg[0], b_re[0], jnp.zeros((pad_cols,), F32)])[None, :]
    p = dict(
        norm1_g=norm1_g[0][None, :], norm2_g=norm2_g[0][None, :],
        w_in=w_in[0].astype(BF16), w_out=w_out[0].astype(BF16),
        conv_dw=conv_dw[0], conv_b=conv_b[0][None, :],
        conv_ln_g=conv_ln_g[0][None, :], conv_ln_b=conv_ln_b[0][None, :],
        q_norm_g=q_norm_g[0][None, :] * (HEAD_DIM ** -0.5 * LOG2_E), k_norm_g=k_norm_g[0][None, :],
        w_route=w_route, b_route=b_route,
        w_gate=w_gate[0].astype(BF16), w_up=w_up[0].astype(BF16), w_down=w_down[0].astype(BF16),
    )

    y_ctx, k_new, v_new = _mixer(x_prompt, lambda tok: ctx_row, mod, None, p)
    attend = functools.partial(_na_attention, cache_k=cache_k, cache_v=cache_v, rpb=rpb[0])
    y_smp, _, _ = _mixer(x_sample, lambda tok: tok // dec_s, mod, attend, p)
    out_ctx, out_smp = _routed_mlp(y_ctx, y_smp, mod, ctx_row, dec_s, p)
    return (out_ctx.reshape(x_prompt.shape), out_smp.reshape(x_sample.shape), k_new, v_new)
```
